```python
import jax, jax.numpy as jnp
from jax import lax
import numpy as np

D_MODEL = 2048
BATCH = 4
SEQ = 4096
DEPTH = 2
DEC_BATCH = 2
DEC_SEQ = 4096
PAST_LEN = 128

GRID_W = 64
D_FF = 5632
RET_HEADS = 8
RET_DK = D_MODEL // RET_HEADS
RET_DV = 2 * RET_DK
RET_VDIM = RET_HEADS * RET_DV
RET_IN = 2 * D_MODEL + 3 * RET_VDIM
RET_CHUNK = 128
ROPE_BASE = 10000.0
NA_HEADS = 16
NA_HEAD_DIM = D_MODEL // NA_HEADS
NA_KH = 8
NA_KW = 16
N_RET_LAYERS = (DEPTH + 1) // 2
N_NA_LAYERS = DEPTH // 2
RMS_EPS = 1e-6
MASK_VALUE = -1e30

kernel_name = "hybrid_retention_natten_macaron_encoder"


def _rmsnorm(x, g):
    xf = x.astype(jnp.float32)
    y = xf * lax.rsqrt(jnp.mean(xf * xf, axis=-1, keepdims=True) + RMS_EPS)
    return (y * g.astype(jnp.float32)).astype(x.dtype)


def _swiglu(h, w_in, w_out):
    g, u = jnp.split(h @ w_in, 2, axis=-1)
    return (jax.nn.silu(g) * u) @ w_out


def _rotary(x):
    S, d = x.shape[1], x.shape[-1]
    inv = ROPE_BASE ** (-jnp.arange(0, d, 2, dtype=jnp.float32) / d)
    ang = jnp.arange(S, dtype=jnp.float32)[:, None] * inv[None, :]
    cos = jnp.cos(ang)[None, :, None, :]
    sin = jnp.sin(ang)[None, :, None, :]
    x1, x2 = jnp.split(x.astype(jnp.float32), 2, axis=-1)
    return jnp.concatenate([x1 * cos - x2 * sin, x1 * sin + x2 * cos], axis=-1)


def _retention_scan(q, k, v, log_gamma):
    B, H, S, dk = q.shape
    dv = v.shape[-1]
    C = RET_CHUNK
    nc = S // C
    pos = jnp.arange(C, dtype=jnp.float32)
    diff = pos[:, None] - pos[None, :]
    inner = jnp.where(diff[None] >= 0, jnp.exp(log_gamma[:, None, None] * jnp.maximum(diff, 0.0)[None]), 0.0)
    cross = jnp.exp(log_gamma[:, None] * (pos + 1.0)[None])[:, :, None]
    into = jnp.exp(log_gamma[:, None] * (C - 1.0 - pos)[None])[:, :, None]
    carry_decay = jnp.exp(log_gamma * C)[:, None, None]

    def to_chunks(t):
        return t.reshape(B, H, nc, C, t.shape[-1]).transpose(2, 0, 1, 3, 4)

    def step(R, inp):
        qi, ki, vi = inp
        s = jnp.einsum('bhnd,bhmd->bhnm', qi, ki) * inner
        o = jnp.einsum('bhnm,bhmv->bhnv', s, vi) + jnp.einsum('bhnd,bhdv->bhnv', qi, R) * cross
        R = R * carry_decay + jnp.einsum('bhmd,bhmv->bhdv', ki * into, vi)
        return R, o

    R0 = jnp.zeros((B, H, dk, dv), jnp.float32)
    _, o = lax.scan(step, R0, (to_chunks(q), to_chunks(k), to_chunks(v)))
    return o.transpose(1, 2, 0, 3, 4).reshape(B, H, S, dv)


def _retention_mixer(h, w_in, w_out, decay_f, decay_b, gn_f, gn_b):
    B, S, _ = h.shape
    q, k, v, gf, gb = jnp.split(h @ w_in, [D_MODEL, 2 * D_MODEL, 2 * D_MODEL + RET_VDIM,
                                           2 * D_MODEL + 2 * RET_VDIM], axis=-1)
    q = _rotary(q.reshape(B, S, RET_HEADS, RET_DK)).transpose(0, 2, 1, 3)
    k = (_rotary(k.reshape(B, S, RET_HEADS, RET_DK)) * (RET_DK ** -0.5)).transpose(0, 2, 1, 3)
    v = v.reshape(B, S, RET_HEADS, RET_DV).astype(jnp.float32).transpose(0, 2, 1, 3)
    lg_f = jnp.log1p(-jnp.exp(decay_f.astype(jnp.float32)))
    lg_b = jnp.log1p(-jnp.exp(decay_b.astype(jnp.float32)))
    o_f = _retention_scan(q, k, v, lg_f)
    o_b = jnp.flip(_retention_scan(jnp.flip(q, 2), jnp.flip(k, 2), jnp.flip(v, 2), lg_b), 2)

    def head_norm(o, g):
        o = o * lax.rsqrt(jnp.mean(o * o, axis=-1, keepdims=True) + RMS_EPS)
        return o.transpose(0, 2, 1, 3).reshape(B, S, RET_VDIM) * g.astype(jnp.float32)

    mixed = (jax.nn.silu(gf.astype(jnp.float32)) * head_norm(o_f, gn_f)
             + jax.nn.silu(gb.astype(jnp.float32)) * head_norm(o_b, gn_b))
    return mixed.astype(h.dtype) @ w_out


def _neighbourhood_attention(h, w_in, w_out, rpb):
    B, S, _ = h.shape
    rows = S // GRID_W
    kh = min(NA_KH, rows)
    kw = NA_KW
    nqb = GRID_W // kw
    q, k, v = jnp.split(h @ w_in, 3, axis=-1)

    def grid(t):
        return t.reshape(B, rows, GRID_W, NA_HEADS, NA_HEAD_DIM)

    q, k, v = grid(q), grid(k), grid(v)
    qcol = np.arange(nqb)[:, None] * kw + np.arange(kw)[None, :]
    kstart = np.clip(np.arange(nqb) * kw - kw // 2, 0, GRID_W - 2 * kw)
    kcol = kstart[:, None] + np.arange(2 * kw)[None, :]
    qwin = np.clip(qcol - kw // 2, 0, GRID_W - kw)
    col_in = jnp.asarray((kcol[:, None, :] >= qwin[:, :, None]) & (kcol[:, None, :] < qwin[:, :, None] + kw))
    dc_idx = jnp.asarray(np.clip(kcol[:, None, :] - qcol[:, :, None] + kw - 1, 0, 2 * kw - 2))
    kcol = jnp.asarray(kcol)
    scale = NA_HEAD_DIM ** -0.5
    rpb_f = rpb.astype(jnp.float32)

    def one_row(r):
        rs = jnp.clip(r - kh // 2, 0, rows - kh)
        kr = lax.dynamic_slice_in_dim(k, rs, kh, axis=1)[:, :, kcol]
        vr = lax.dynamic_slice_in_dim(v, rs, kh, axis=1)[:, :, kcol]
        qr = lax.dynamic_index_in_dim(q, r, axis=1, keepdims=False).reshape(B, nqb, kw, NA_HEADS, NA_HEAD_DIM)
        dr_idx = rs + jnp.arange(kh) - r + NA_KH - 1
        bias = rpb_f[:, dr_idx[None, None, :, None], dc_idx[:, :, None, :]]
        s = jnp.einsum('bjqhd,brjkhd->bhjqrk', qr, kr).astype(jnp.float32) * scale + bias
        s = jnp.where(col_in[:, :, None, :], s, MASK_VALUE)
        p = jax.nn.softmax(s, axis=(-2, -1))
        o = jnp.einsum('bhjqrk,brjkhd->bjqhd', p.astype(v.dtype), vr)
        return o.reshape(B, GRID_W, D_MODEL)

    out = lax.map(one_row, jnp.arange(rows))
    out = out.transpose(1, 0, 2, 3).reshape(B, S, D_MODEL)
    return out @ w_out


def _trunk(x, norm_ffn1, ffn1_w_in, ffn1_w_out, norm_mix, norm_ffn2, ffn2_w_in, ffn2_w_out,
           ret_w_in, ret_w_out, ret_decay_f, ret_decay_b, ret_gn_f, ret_gn_b,
           na_w_in, na_w_out, na_rpb, norm_final):
    for i in range(DEPTH):
        x = x + 0.5 * _swiglu(_rmsnorm(x, norm_ffn1[i]), ffn1_w_in[i], ffn1_w_out[i])
        h = _rmsnorm(x, norm_mix[i])
        j = i // 2
        if i % 2 == 0:
            x = x + _retention_mixer(h, ret_w_in[j], ret_w_out[j], ret_decay_f[j], ret_decay_b[j],
                                     ret_gn_f[j], ret_gn_b[j])
        else:
            x = x + _neighbourhood_attention(h, na_w_in[j], na_w_out[j], na_rpb[j])
        x = x + 0.5 * _swiglu(_rmsnorm(x, norm_ffn2[i]), ffn2_w_in[i], ffn2_w_out[i])
    return _rmsnorm(x, norm_final)


def setup_inputs(seed: int = 0) -> dict:
    key = jax.random.key(seed)
    ks = jax.random.split(key, 24)
    f32 = jnp.float32
    n = jax.random.normal
    ln2 = float(np.log(2.0))
    h_idx = jnp.arange(RET_HEADS, dtype=f32)
    return {
        'x_prompt': n(ks[0], (BATCH, SEQ, D_MODEL), f32),
        'x_sample': n(ks[1], (DEC_BATCH, DEC_SEQ, D_MODEL), f32),
        'norm_ffn1': 1.0 + 0.02 * n(ks[2], (DEPTH, D_MODEL), f32),
        'ffn1_w_in': n(ks[3], (DEPTH, D_MODEL, 2 * D_FF), f32) * D_MODEL ** -0.5,
        'ffn1_w_out': n(ks[4], (DEPTH, D_FF, D_MODEL), f32) * D_FF ** -0.5,
        'norm_mix': 1.0 + 0.02 * n(ks[5], (DEPTH, D_MODEL), f32),
        'norm_ffn2': 1.0 + 0.02 * n(ks[6], (DEPTH, D_MODEL), f32),
        'ffn2_w_in': n(ks[7], (DEPTH, D_MODEL, 2 * D_FF), f32) * D_MODEL ** -0.5,
        'ffn2_w_out': n(ks[8], (DEPTH, D_FF, D_MODEL), f32) * D_FF ** -0.5,
        'ret_w_in': n(ks[9], (N_RET_LAYERS, D_MODEL, RET_IN), f32) * D_MODEL ** -0.5,
        'ret_w_out': n(ks[10], (N_RET_LAYERS, RET_VDIM, D_MODEL), f32) * RET_VDIM ** -0.5,
        'ret_decay_f': -(5.0 + h_idx)[None, :] * ln2 + 0.01 * n(ks[11], (N_RET_LAYERS, RET_HEADS), f32),
        'ret_decay_b': -(5.0 + (RET_HEADS - 1.0 - h_idx))[None, :] * ln2 + 0.01 * n(ks[12], (N_RET_LAYERS, RET_HEADS), f32),
        'ret_gn_f': 1.0 + 0.02 * n(ks[13], (N_RET_LAYERS, RET_VDIM), f32),
        'ret_gn_b': 1.0 + 0.02 * n(ks[14], (N_RET_LAYERS, RET_VDIM), f32),
        'na_w_in': n(ks[15], (N_NA_LAYERS, D_MODEL, 3 * D_MODEL), f32) * D_MODEL ** -0.5,
        'na_w_out': n(ks[16], (N_NA_LAYERS, D_MODEL, D_MODEL), f32) * D_MODEL ** -0.5,
        'na_rpb': 0.02 * n(ks[17], (N_NA_LAYERS, NA_HEADS, 2 * NA_KH - 1, 2 * NA_KW - 1), f32),
        'norm_final': 1.0 + 0.02 * n(ks[18], (D_MODEL,), f32),
    }


def reference(x_prompt, x_sample, norm_ffn1, ffn1_w_in, ffn1_w_out, norm_mix, norm_ffn2, ffn2_w_in,
              ffn2_w_out, ret_w_in, ret_w_out, ret_decay_f, ret_decay_b, ret_gn_f, ret_gn_b,
              na_w_in, na_w_out, na_rpb, norm_final):
    y_prompt = _trunk(x_prompt, norm_ffn1, ffn1_w_in, ffn1_w_out, norm_mix, norm_ffn2, ffn2_w_in, ffn2_w_out,
                      ret_w_in, ret_w_out, ret_decay_f, ret_decay_b, ret_gn_f, ret_gn_b,
                      na_w_in, na_w_out, na_rpb, norm_final)
    y_sample = _trunk(x_sample, norm_ffn1, ffn1_w_in, ffn1_w_out, norm_mix, norm_ffn2, ffn2_w_in, ffn2_w_out,
                      ret_w_in, ret_w_out, ret_decay_f, ret_decay_b, ret_gn_f, ret_gn_b,
                      na_w_in, na_w_out, na_rpb, norm_final)
    return (y_prompt, y_sample)
```

```python
import functools

import numpy as np
import jax
import jax.numpy as jnp
from jax import lax
from jax.experimental import pallas as pl
from jax.experimental.pallas import tpu as pltpu

D_MODEL = 2048
SEQ = 4096
DEPTH = 2
GRID_W = 64
D_FF = 5632
RET_HEADS = 8
RET_DK = D_MODEL // RET_HEADS
RET_DV = 2 * RET_DK
RET_VDIM = RET_HEADS * RET_DV
RET_IN = 2 * D_MODEL + 3 * RET_VDIM
RET_CHUNK = 128
ROPE_BASE = 10000.0
NA_HEADS = 16
NA_HEAD_DIM = D_MODEL // NA_HEADS
NA_KH = 8
NA_KW = 16
RMS_EPS = 1e-6
MASK_VALUE = -1e30

F32 = jnp.float32
BF16 = jnp.bfloat16

V7X_VMEM_BYTES = 64 * 1024 * 1024
VMEM_LIMIT_BYTES = V7X_VMEM_BYTES - 8 * 1024 * 1024

FFN_TM, FFN_TF = 512, 512
PROJ_TM, PROJ_TN = 1024, 1024
OUT_TM, OUT_TN = 1024, 512
SCAN_TC = 512
GRID_ROWS = SEQ // GRID_W
NA_SLAB = NA_KH * GRID_W


def _params(semantics):
    return pltpu.CompilerParams(dimension_semantics=semantics, vmem_limit_bytes=VMEM_LIMIT_BYTES)


def _rmsnorm(xf, g):
    ms = jnp.mean(xf * xf, axis=-1, keepdims=True)
    return xf * lax.rsqrt(ms + RMS_EPS) * g


def _silu(g):
    return g / (1.0 + jnp.exp(-g))


def _ffn_kernel(x_ref, g_ref, wg_ref, wu_ref, wo_ref, *rest, final_norm):
    if final_norm:
        gfin_ref, o_ref, h_ref = rest
    else:
        o_ref, h_ref = rest
    j = pl.program_id(1)

    @pl.when(j == 0)
    def _():
        h_ref[...] = _rmsnorm(x_ref[...], g_ref[...]).astype(BF16)
        o_ref[...] = jnp.zeros_like(o_ref)

    h = h_ref[...]
    g = jnp.dot(h, wg_ref[...], preferred_element_type=F32)
    u = jnp.dot(h, wu_ref[...], preferred_element_type=F32)
    a = (_silu(g) * u).astype(BF16)
    o_ref[...] += jnp.dot(a, wo_ref[...], preferred_element_type=F32)

    @pl.when(j == pl.num_programs(1) - 1)
    def _():
        y = x_ref[...] + 0.5 * o_ref[...]
        if final_norm:
            y = _rmsnorm(y, gfin_ref[...])
        o_ref[...] = y


def _ffn(x, gain, w_in, w_out, final_gain=None):
    t = x.shape[0]
    nj = D_FF // FFN_TF
    in_specs = [
        pl.BlockSpec((FFN_TM, D_MODEL), lambda i, j: (i, 0)),
        pl.BlockSpec((1, D_MODEL), lambda i, j: (0, 0)),
        pl.BlockSpec((D_MODEL, FFN_TF), lambda i, j: (0, j)),
        pl.BlockSpec((D_MODEL, FFN_TF), lambda i, j: (0, nj + j)),
        pl.BlockSpec((FFN_TF, D_MODEL), lambda i, j: (j, 0)),
    ]
    args = [x, gain.reshape(1, D_MODEL), w_in, w_in, w_out]
    if final_gain is not None:
        in_specs.append(pl.BlockSpec((1, D_MODEL), lambda i, j: (0, 0)))
        args.append(final_gain.reshape(1, D_MODEL))
    return pl.pallas_call(
        functools.partial(_ffn_kernel, final_norm=final_gain is not None),
        grid=(t // FFN_TM, nj),
        in_specs=in_specs,
        out_specs=pl.BlockSpec((FFN_TM, D_MODEL), lambda i, j: (i, 0)),
        out_shape=jax.ShapeDtypeStruct((t, D_MODEL), F32),
        scratch_shapes=[pltpu.VMEM((FFN_TM, D_MODEL), BF16)],
        compiler_params=_params(("parallel", "arbitrary")),
        name="swiglu_ffn",
    )(*args)


def _proj_kernel(x_ref, g_ref, w_ref, *rest, rot_tiles):
    if rot_tiles:
        cos_ref, sin_ref, o_ref, h_ref = rest
    else:
        o_ref, h_ref = rest
    n = pl.program_id(1)

    @pl.when(n == 0)
    def _():
        h_ref[...] = _rmsnorm(x_ref[...], g_ref[...]).astype(BF16)

    y = jnp.dot(h_ref[...], w_ref[...], preferred_element_type=F32)
    if not rot_tiles:
        o_ref[...] = y.astype(BF16)
        return

    @pl.when(n < rot_tiles)
    def _():
        cos = cos_ref[...]
        sin = sin_ref[...]
        scale = jnp.where(n < rot_tiles // 2, 1.0, RET_DK ** -0.5).astype(F32)
        half = RET_DK // 2
        for hh in range(PROJ_TN // RET_DK):
            x1 = y[:, hh * RET_DK: hh * RET_DK + half]
            x2 = y[:, hh * RET_DK + half: (hh + 1) * RET_DK]
            o_ref[:, hh * RET_DK: hh * RET_DK + half] = ((x1 * cos - x2 * sin) * scale).astype(BF16)
            o_ref[:, hh * RET_DK + half: (hh + 1) * RET_DK] = ((x1 * sin + x2 * cos) * scale).astype(BF16)

    @pl.when(n >= rot_tiles)
    def _():
        o_ref[...] = y.astype(BF16)


def _proj(x, gain, w, rotary=None):
    t = x.shape[0]
    n_out = w.shape[1]
    in_specs = [
        pl.BlockSpec((PROJ_TM, D_MODEL), lambda i, n: (i, 0)),
        pl.BlockSpec((1, D_MODEL), lambda i, n: (0, 0)),
        pl.BlockSpec((D_MODEL, PROJ_TN), lambda i, n: (0, n)),
    ]
    args = [x, gain.reshape(1, D_MODEL), w]
    rot_tiles = 0
    if rotary is not None:
        rot_tiles = 2 * D_MODEL // PROJ_TN
        seq_tiles = SEQ // PROJ_TM
        tab_spec = pl.BlockSpec((PROJ_TM, RET_DK // 2), lambda i, n: (i % seq_tiles, 0))
        in_specs += [tab_spec, tab_spec]
        args += list(rotary)
    return pl.pallas_call(
        functools.partial(_proj_kernel, rot_tiles=rot_tiles),
        grid=(t // PROJ_TM, n_out // PROJ_TN),
        in_specs=in_specs,
        out_specs=pl.BlockSpec((PROJ_TM, PROJ_TN), lambda i, n: (i, n)),
        out_shape=jax.ShapeDtypeStruct((t, n_out), BF16),
        scratch_shapes=[pltpu.VMEM((PROJ_TM, D_MODEL), BF16)],
        compiler_params=_params(("parallel", "arbitrary")),
        name="norm_proj",
    )(*args)


def _out_kernel(a_ref, w_ref, x_ref, o_ref):
    o_ref[...] = x_ref[...] + jnp.dot(a_ref[...], w_ref[...], preferred_element_type=F32)


def _out_proj(a, w, x):
    t, k = a.shape
    return pl.pallas_call(
        _out_kernel,
        grid=(t // OUT_TM, D_MODEL // OUT_TN),
        in_specs=[
            pl.BlockSpec((OUT_TM, k), lambda i, n: (i, 0)),
            pl.BlockSpec((k, OUT_TN), lambda i, n: (0, n)),
            pl.BlockSpec((OUT_TM, OUT_TN), lambda i, n: (i, n)),
        ],
        out_specs=pl.BlockSpec((OUT_TM, OUT_TN), lambda i, n: (i, n)),
        out_shape=jax.ShapeDtypeStruct((t, D_MODEL), F32),
        compiler_params=_params(("parallel", "arbitrary")),
        name="out_proj_residual",
    )(a, w, x)


SCAN_NBLK = SEQ // SCAN_TC
SCAN_CHUNKS = SCAN_TC // RET_CHUNK


def _scan_kernel(dec_ref, q_ref, k_ref, v_ref, gf_ref, gb_ref, tab_ref, gnf_ref, gnb_ref,
                 o_ref, r_ref, ob_ref):
    h = pl.program_id(1)
    t = pl.program_id(2)

    @pl.when((t == 0) | (t == SCAN_NBLK))
    def _():
        r_ref[...] = jnp.zeros_like(r_ref)

    def run(backward):
        inner = tab_ref[0]
        cross = jnp.concatenate([tab_ref[1]] * (RET_DV // RET_CHUNK), axis=1)
        into = jnp.concatenate([tab_ref[2]] * (RET_DK // RET_CHUNK), axis=1)
        dec = dec_ref[1 if backward else 0, h]
        gate_ref = gb_ref if backward else gf_ref
        gn = (gnb_ref if backward else gnf_ref)[...]
        blk = (SCAN_NBLK - 1 - t) if backward else (t - SCAN_NBLK)
        order = range(SCAN_CHUNKS - 1, -1, -1) if backward else range(SCAN_CHUNKS)
        for c in order:
            rows = pl.ds(c * RET_CHUNK, RET_CHUNK)
            q = q_ref[rows, :]
            k = k_ref[rows, :]
            v = v_ref[rows, :]
            s = lax.dot_general(q, k, (((1,), (1,)), ((), ())), preferred_element_type=F32) * inner
            r = r_ref[...]
            o = (jnp.dot(s.astype(BF16), v, preferred_element_type=F32)
                 + jnp.dot(q, r.astype(BF16), preferred_element_type=F32) * cross)
            kd = (k.astype(F32) * into).astype(BF16)
            r_ref[...] = r * dec + lax.dot_general(kd, v, (((0,), (0,)), ((), ())),
                                                   preferred_element_type=F32)
            on = o * lax.rsqrt(jnp.mean(o * o, axis=-1, keepdims=True) + RMS_EPS) * gn
            m = _silu(gate_ref[rows, :].astype(F32)) * on
            ob_rows = pl.ds(pl.multiple_of(blk * SCAN_TC + c * RET_CHUNK, RET_CHUNK), RET_CHUNK)
            if backward:
                ob_ref[ob_rows, :] = m
            else:
                o_ref[rows, :] = (m + ob_ref[ob_rows, :]).astype(BF16)

    @pl.when(t < SCAN_NBLK)
    def _():
        o_ref[...] = jnp.zeros_like(o_ref)
        run(True)

    @pl.when(t >= SCAN_NBLK)
    def _():
        run(False)


def _retention_scan(p, tabs, dec, gn_f, gn_b):
    t = p.shape[0]
    batch = t // SEQ
    nb = SCAN_NBLK

    def bwd_blk(s):
        return jnp.where(s < nb, nb - 1 - s, 0)

    def fwd_blk(s):
        return jnp.where(s < nb, 0, s - nb)

    def cur_blk(s):
        return jnp.where(s < nb, nb - 1 - s, s - nb)

    qk_w = RET_DK
    v_w = RET_DV
    k_off = D_MODEL // qk_w
    v_off = 2 * D_MODEL // v_w
    gf_off = v_off + RET_HEADS
    gb_off = gf_off + RET_HEADS
    in_specs = [
        pl.BlockSpec(memory_space=pltpu.SMEM),
        pl.BlockSpec((SCAN_TC, qk_w), lambda b, h, s: (b * nb + cur_blk(s), h)),
        pl.BlockSpec((SCAN_TC, qk_w), lambda b, h, s: (b * nb + cur_blk(s), k_off + h)),
        pl.BlockSpec((SCAN_TC, v_w), lambda b, h, s: (b * nb + cur_blk(s), v_off + h)),
        pl.BlockSpec((SCAN_TC, v_w), lambda b, h, s: (b * nb + fwd_blk(s), gf_off + h)),
        pl.BlockSpec((SCAN_TC, v_w), lambda b, h, s: (b * nb + bwd_blk(s), gb_off + h)),
        pl.BlockSpec((None, None, 3, RET_CHUNK, RET_CHUNK),
                     lambda b, h, s: (jnp.where(s < nb, 1, 0), h, 0, 0, 0)),
        pl.BlockSpec((1, v_w), lambda b, h, s: (0, h)),
        pl.BlockSpec((1, v_w), lambda b, h, s: (0, h)),
    ]
    return pl.pallas_call(
        _scan_kernel,
        grid=(batch, RET_HEADS, 2 * nb),
        in_specs=in_specs,
        out_specs=pl.BlockSpec((SCAN_TC, v_w), lambda b, h, s: (b * nb + fwd_blk(s), h)),
        out_shape=jax.ShapeDtypeStruct((t, RET_VDIM), BF16),
        scratch_shapes=[pltpu.VMEM((RET_DK, RET_DV), F32), pltpu.VMEM((SEQ, RET_DV), F32)],
        compiler_params=_params(("arbitrary", "arbitrary", "arbitrary")),
        name="retention_scan",
    )(dec, p, p, p, p, p, tabs, gn_f.reshape(1, RET_VDIM), gn_b.reshape(1, RET_VDIM))


def _retention_tables(decay_f, decay_b):
    c = RET_CHUNK
    pos = jnp.arange(c, dtype=F32)
    ones = jnp.ones((1, 1, c), F32)

    def tables(log_gamma, backward):
        lg = log_gamma[:, None, None]
        diff = pos[:, None] - pos[None, :]
        if backward:
            diff = -diff
        inner = jnp.where(diff[None] >= 0, jnp.exp(lg * jnp.maximum(diff, 0.0)[None]), 0.0)
        cross_pos = (c - pos) if backward else (pos + 1.0)
        into_pos = pos if backward else (c - 1.0 - pos)
        cross = jnp.exp(log_gamma[:, None] * cross_pos[None])[:, :, None] * ones
        into = jnp.exp(log_gamma[:, None] * into_pos[None])[:, :, None] * ones
        return jnp.stack([inner, cross, into], axis=1), jnp.exp(log_gamma * c)

    lg_f = jnp.log1p(-jnp.exp(decay_f.astype(F32)))
    lg_b = jnp.log1p(-jnp.exp(decay_b.astype(F32)))
    tab_f, dec_f = tables(lg_f, False)
    tab_b, dec_b = tables(lg_b, True)
    return jnp.stack([tab_f, tab_b]), jnp.stack([dec_f, dec_b])


def _rotary_tables():
    d = RET_DK
    inv = ROPE_BASE ** (-jnp.arange(0, d, 2, dtype=F32) / d)
    ang = jnp.arange(SEQ, dtype=F32)[:, None] * inv[None, :]
    return jnp.cos(ang), jnp.sin(ang)


def _na_kernel(q_ref, k_ref, v_ref, bias_ref, o_ref):
    scale = NA_HEAD_DIM ** -0.5

    def row(r, carry):
        rs = jnp.clip(r - NA_KH // 2, 0, GRID_ROWS - NA_KH)
        q = q_ref[pl.ds(pl.multiple_of(r * GRID_W, GRID_W), GRID_W), :]
        slab = pl.ds(pl.multiple_of(rs * GRID_W, GRID_W), NA_SLAB)
        s = lax.dot_general(q, k_ref[slab, :], (((1,), (1,)), ((), ())), preferred_element_type=F32)
        s = s * scale + bias_ref[r - rs]
        e = jnp.exp(s - jnp.max(s, axis=-1, keepdims=True))
        p = e / jnp.sum(e, axis=-1, keepdims=True)
        o = jnp.dot(p.astype(BF16), v_ref[slab, :], preferred_element_type=F32)
        o_ref[pl.ds(pl.multiple_of(r * GRID_W, GRID_W), GRID_W), :] = o.astype(BF16)
        return carry

    lax.fori_loop(0, GRID_ROWS, row, 0)


def _na_bias_tables(rpb):
    cq = np.arange(GRID_W)
    ck = np.arange(GRID_W)
    qwin = np.clip(cq - NA_KW // 2, 0, GRID_W - NA_KW)
    col_ok = (ck[None, :] >= qwin[:, None]) & (ck[None, :] < qwin[:, None] + NA_KW)
    dc = np.clip(ck[None, :] - cq[:, None] + NA_KW - 1, 0, 2 * NA_KW - 2)
    dr = np.arange(NA_KH)[None, :] - np.arange(NA_KH)[:, None] + NA_KH - 1
    bias = rpb.astype(F32)[:, dr[:, :, None, None], dc[None, None, :, :]]
    bias = jnp.where(col_ok[None, None, None], bias, MASK_VALUE)
    return bias.transpose(0, 1, 3, 2, 4).reshape(NA_HEADS, NA_KH, GRID_W, NA_SLAB)


def _neighbourhood_attention(qkv, bias):
    t = qkv.shape[0]
    batch = t // SEQ
    hd = NA_HEAD_DIM
    return pl.pallas_call(
        _na_kernel,
        grid=(NA_HEADS, batch),
        in_specs=[
            pl.BlockSpec((SEQ, hd), lambda h, b: (b, h)),
            pl.BlockSpec((SEQ, hd), lambda h, b: (b, NA_HEADS + h)),
            pl.BlockSpec((SEQ, hd), lambda h, b: (b, 2 * NA_HEADS + h)),
            pl.BlockSpec((None, NA_KH, GRID_W, NA_SLAB), lambda h, b: (h, 0, 0, 0)),
        ],
        out_specs=pl.BlockSpec((SEQ, hd), lambda h, b: (b, h)),
        out_shape=jax.ShapeDtypeStruct((t, D_MODEL), BF16),
        compiler_params=_params(("arbitrary", "arbitrary")),
        name="neighbourhood_attention",
    )(qkv, qkv, qkv, bias)


def kernel(x_prompt, x_sample, norm_ffn1, ffn1_w_in, ffn1_w_out, norm_mix, norm_ffn2, ffn2_w_in, ffn2_w_out,
           ret_w_in, ret_w_out, ret_decay_f, ret_decay_b, ret_gn_f, ret_gn_b, na_w_in, na_w_out, na_rpb,
           norm_final):
    n_prompt = x_prompt.shape[0]
    x = jnp.concatenate([x_prompt, x_sample], axis=0).reshape(-1, D_MODEL)
    bf = lambda w: w.astype(BF16)
    rotary = _rotary_tables()
    for i in range(DEPTH):
        x = _ffn(x, norm_ffn1[i], bf(ffn1_w_in[i]), bf(ffn1_w_out[i]))
        j = i // 2
        if i % 2 == 0:
            tabs, dec = _retention_tables(ret_decay_f[j], ret_decay_b[j])
            p = _proj(x, norm_mix[i], bf(ret_w_in[j]), rotary=rotary)
            mixed = _retention_scan(p, tabs, dec, ret_gn_f[j], ret_gn_b[j])
            x = _out_proj(mixed, bf(ret_w_out[j]), x)
        else:
            qkv = _proj(x, norm_mix[i], bf(na_w_in[j]))
            att = _neighbourhood_attention(qkv, _na_bias_tables(na_rpb[j]))
            x = _out_proj(att, bf(na_w_out[j]), x)
        x = _ffn(x, norm_ffn2[i], bf(ffn2_w_in[i]), bf(ffn2_w_out[i]),
                 final_gain=norm_final if i == DEPTH - 1 else None)
    y = x.reshape(-1, SEQ, D_MODEL)
    return y[:n_prompt], y[n_prompt:]
```

```python
import functools

import numpy as np
import jax
import jax.numpy as jnp
from jax import lax
from jax.experimental import pallas as pl
from jax.experimental.pallas import tpu as pltpu

D_MODEL = 2048
SEQ = 4096
DEPTH = 2
GRID_W = 64
D_FF = 5632
RET_HEADS = 8
RET_DK = D_MODEL // RET_HEADS
RET_DV = 2 * RET_DK
RET_VDIM = RET_HEADS * RET_DV
ROPE_BASE = 10000.0
NA_HEADS = 16
NA_HEAD_DIM = D_MODEL // NA_HEADS
NA_KH = 8
NA_KW = 16
RMS_EPS = 1e-6
MASK_VALUE = -1e30

F32 = jnp.float32
BF16 = jnp.bfloat16

V7X_VMEM_BYTES = 64 * 1024 * 1024
VMEM_LIMIT_BYTES = V7X_VMEM_BYTES - 8 * 1024 * 1024

FFN_TM, FFN_TF = 512, 512
PROJ_TM, PROJ_TN = 1024, 1024
OUT_TM, OUT_TN = 1024, 512
SCAN_TC = 512
SCAN_CHUNK = 256
GRID_ROWS = SEQ // GRID_W


def _params(semantics):
    return pltpu.CompilerParams(dimension_semantics=semantics, vmem_limit_bytes=VMEM_LIMIT_BYTES)


def _rmsnorm(xf, g):
    ms = jnp.mean(xf * xf, axis=-1, keepdims=True)
    return xf * lax.rsqrt(ms + RMS_EPS) * g


def _silu(g):
    return g / (1.0 + jnp.exp(-g))


def _ffn_kernel(x_ref, g_ref, wg_ref, wu_ref, wo_ref, *rest, final_norm):
    if final_norm:
        gfin_ref, o_ref, h_ref = rest
    else:
        o_ref, h_ref = rest
    j = pl.program_id(1)

    @pl.when(j == 0)
    def _():
        h_ref[...] = _rmsnorm(x_ref[...], g_ref[...]).astype(BF16)
        o_ref[...] = jnp.zeros_like(o_ref)

    h = h_ref[...]
    g = jnp.dot(h, wg_ref[...], preferred_element_type=F32)
    u = jnp.dot(h, wu_ref[...], preferred_element_type=F32)
    a = (_silu(g) * u).astype(BF16)
    o_ref[...] += jnp.dot(a, wo_ref[...], preferred_element_type=F32)

    @pl.when(j == pl.num_programs(1) - 1)
    def _():
        y = x_ref[...] + 0.5 * o_ref[...]
        if final_norm:
            y = _rmsnorm(y, gfin_ref[...])
        o_ref[...] = y


def _ffn(x, gain, w_in, w_out, final_gain=None):
    t = x.shape[0]
    nj = D_FF // FFN_TF
    in_specs = [
        pl.BlockSpec((FFN_TM, D_MODEL), lambda i, j: (i, 0)),
        pl.BlockSpec((1, D_MODEL), lambda i, j: (0, 0)),
        pl.BlockSpec((D_MODEL, FFN_TF), lambda i, j: (0, j)),
        pl.BlockSpec((D_MODEL, FFN_TF), lambda i, j: (0, nj + j)),
        pl.BlockSpec((FFN_TF, D_MODEL), lambda i, j: (j, 0)),
    ]
    args = [x, gain.reshape(1, D_MODEL), w_in, w_in, w_out]
    if final_gain is not None:
        in_specs.append(pl.BlockSpec((1, D_MODEL), lambda i, j: (0, 0)))
        args.append(final_gain.reshape(1, D_MODEL))
    return pl.pallas_call(
        functools.partial(_ffn_kernel, final_norm=final_gain is not None),
        grid=(t // FFN_TM, nj),
        in_specs=in_specs,
        out_specs=pl.BlockSpec((FFN_TM, D_MODEL), lambda i, j: (i, 0)),
        out_shape=jax.ShapeDtypeStruct((t, D_MODEL), F32),
        scratch_shapes=[pltpu.VMEM((FFN_TM, D_MODEL), BF16)],
        compiler_params=_params(("parallel", "arbitrary")),
        name="swiglu_ffn",
    )(*args)


def _proj_kernel(x_ref, g_ref, w_ref, *rest, rotary):
    if rotary:
        cos_ref, sin_ref, o_ref, h_ref = rest
    else:
        o_ref, h_ref = rest
    n = pl.program_id(1)

    @pl.when(n == 0)
    def _():
        h_ref[...] = _rmsnorm(x_ref[...], g_ref[...]).astype(BF16)

    y = jnp.dot(h_ref[...], w_ref[...], preferred_element_type=F32)
    if not rotary:
        o_ref[...] = y.astype(BF16)
        return
    cos = cos_ref[...]
    sin = sin_ref[...]
    scale = jnp.where(n < pl.num_programs(1) // 2, 1.0, RET_DK ** -0.5).astype(F32)
    half = RET_DK // 2
    for hh in range(PROJ_TN // RET_DK):
        x1 = y[:, hh * RET_DK: hh * RET_DK + half]
        x2 = y[:, hh * RET_DK + half: (hh + 1) * RET_DK]
        o_ref[:, hh * RET_DK: hh * RET_DK + half] = ((x1 * cos - x2 * sin) * scale).astype(BF16)
        o_ref[:, hh * RET_DK + half: (hh + 1) * RET_DK] = ((x1 * sin + x2 * cos) * scale).astype(BF16)


def _proj(x, gain, w, rotary=None):
    t = x.shape[0]
    n_out = w.shape[1]
    in_specs = [
        pl.BlockSpec((PROJ_TM, D_MODEL), lambda i, n: (i, 0)),
        pl.BlockSpec((1, D_MODEL), lambda i, n: (0, 0)),
        pl.BlockSpec((D_MODEL, PROJ_TN), lambda i, n: (0, n)),
    ]
    args = [x, gain.reshape(1, D_MODEL), w]
    if rotary is not None:
        seq_tiles = SEQ // PROJ_TM
        tab_spec = pl.BlockSpec((PROJ_TM, RET_DK // 2), lambda i, n: (i % seq_tiles, 0))
        in_specs += [tab_spec, tab_spec]
        args += list(rotary)
    return pl.pallas_call(
        functools.partial(_proj_kernel, rotary=rotary is not None),
        grid=(t // PROJ_TM, n_out // PROJ_TN),
        in_specs=in_specs,
        out_specs=pl.BlockSpec((PROJ_TM, PROJ_TN), lambda i, n: (i, n)),
        out_shape=jax.ShapeDtypeStruct((t, n_out), BF16),
        scratch_shapes=[pltpu.VMEM((PROJ_TM, D_MODEL), BF16)],
        compiler_params=_params(("parallel", "arbitrary")),
        name="norm_proj_rotary" if rotary is not None else "norm_proj",
    )(*args)


def _out_kernel(a_ref, w_ref, x_ref, o_ref):
    o_ref[...] = x_ref[...] + jnp.dot(a_ref[...], w_ref[...], preferred_element_type=F32)


def _out_proj(a, w, x):
    t, k = a.shape
    return pl.pallas_call(
        _out_kernel,
        grid=(t // OUT_TM, D_MODEL // OUT_TN),
        in_specs=[
            pl.BlockSpec((OUT_TM, k), lambda i, n: (i, 0)),
            pl.BlockSpec((k, OUT_TN), lambda i, n: (0, n)),
            pl.BlockSpec((OUT_TM, OUT_TN), lambda i, n: (i, n)),
        ],
        out_specs=pl.BlockSpec((OUT_TM, OUT_TN), lambda i, n: (i, n)),
        out_shape=jax.ShapeDtypeStruct((t, D_MODEL), F32),
        compiler_params=_params(("parallel", "arbitrary")),
        name="out_proj_residual",
    )(a, w, x)


SCAN_NBLK = SEQ // SCAN_TC
SCAN_CHUNKS = SCAN_TC // SCAN_CHUNK


def _scan_kernel(dec_ref, q_ref, k_ref, v_ref, gf_ref, gb_ref, tab_ref, gnf_ref, gnb_ref,
                 o_ref, r_ref, ob_ref):
    h = pl.program_id(1)
    t = pl.program_id(2)
    c_len = SCAN_CHUNK

    @pl.when((t == 0) | (t == SCAN_NBLK))
    def _():
        r_ref[...] = jnp.zeros_like(r_ref)

    def run(backward):
        inner = tab_ref[0]
        cross = jnp.concatenate([tab_ref[1]] * (RET_DV // c_len), axis=1)
        into = jnp.concatenate([tab_ref[2]] * (RET_DK // c_len), axis=1)
        dec = dec_ref[1 if backward else 0, h]
        gate_ref = gb_ref if backward else gf_ref
        gn = (gnb_ref if backward else gnf_ref)[...]
        blk = (SCAN_NBLK - 1 - t) if backward else (t - SCAN_NBLK)
        order = range(SCAN_CHUNKS - 1, -1, -1) if backward else range(SCAN_CHUNKS)
        for c in order:
            rows = pl.ds(c * c_len, c_len)
            q = q_ref[rows, :]
            k = k_ref[rows, :]
            v = v_ref[rows, :]
            s = lax.dot_general(q, k, (((1,), (1,)), ((), ())), preferred_element_type=F32) * inner
            kd = (k.astype(F32) * into).astype(BF16)
            r = r_ref[...]
            sv = jnp.dot(jnp.concatenate([s.astype(BF16), kd.T], axis=0), v, preferred_element_type=F32)
            o = sv[:c_len] + jnp.dot(q, r.astype(BF16), preferred_element_type=F32) * cross
            r_ref[...] = r * dec + sv[c_len:]
            on = o * lax.rsqrt(jnp.mean(o * o, axis=-1, keepdims=True) + RMS_EPS) * gn
            m = _silu(gate_ref[rows, :].astype(F32)) * on
            ob_rows = pl.ds(pl.multiple_of(blk * SCAN_TC + c * c_len, c_len), c_len)
            if backward:
                ob_ref[ob_rows, :] = m
            else:
                o_ref[rows, :] = (m + ob_ref[ob_rows, :]).astype(BF16)

    @pl.when(t < SCAN_NBLK)
    def _():
        o_ref[...] = jnp.zeros_like(o_ref)
        run(True)

    @pl.when(t >= SCAN_NBLK)
    def _():
        run(False)


def _retention_scan(qk, vg, tabs, dec, gn_f, gn_b):
    t = qk.shape[0]
    batch = t // SEQ
    nb = SCAN_NBLK

    def bwd_blk(s):
        return jnp.where(s < nb, nb - 1 - s, 0)

    def fwd_blk(s):
        return jnp.where(s < nb, 0, s - nb)

    def cur_blk(s):
        return jnp.where(s < nb, nb - 1 - s, s - nb)

    nh = RET_HEADS
    in_specs = [
        pl.BlockSpec(memory_space=pltpu.SMEM),
        pl.BlockSpec((SCAN_TC, RET_DK), lambda b, h, s: (b * nb + cur_blk(s), h)),
        pl.BlockSpec((SCAN_TC, RET_DK), lambda b, h, s: (b * nb + cur_blk(s), nh + h)),
        pl.BlockSpec((SCAN_TC, RET_DV), lambda b, h, s: (b * nb + cur_blk(s), h)),
        pl.BlockSpec((SCAN_TC, RET_DV), lambda b, h, s: (b * nb + fwd_blk(s), nh + h)),
        pl.BlockSpec((SCAN_TC, RET_DV), lambda b, h, s: (b * nb + bwd_blk(s), 2 * nh + h)),
        pl.BlockSpec((None, None, 3, SCAN_CHUNK, SCAN_CHUNK),
                     lambda b, h, s: (jnp.where(s < nb, 1, 0), h, 0, 0, 0)),
        pl.BlockSpec((1, RET_DV), lambda b, h, s: (0, h)),
        pl.BlockSpec((1, RET_DV), lambda b, h, s: (0, h)),
    ]
    return pl.pallas_call(
        _scan_kernel,
        grid=(batch, RET_HEADS, 2 * nb),
        in_specs=in_specs,
        out_specs=pl.BlockSpec((SCAN_TC, RET_DV), lambda b, h, s: (b * nb + fwd_blk(s), h)),
        out_shape=jax.ShapeDtypeStruct((t, RET_VDIM), BF16),
        scratch_shapes=[pltpu.VMEM((RET_DK, RET_DV), F32), pltpu.VMEM((SEQ, RET_DV), F32)],
        compiler_params=_params(("arbitrary", "arbitrary", "arbitrary")),
        name="retention_scan",
    )(dec, qk, qk, vg, vg, vg, tabs, gn_f.reshape(1, RET_VDIM), gn_b.reshape(1, RET_VDIM))


def _retention_tables(decay_f, decay_b):
    c = SCAN_CHUNK
    pos = jnp.arange(c, dtype=F32)
    ones = jnp.ones((1, 1, c), F32)

    def tables(log_gamma, backward):
        lg = log_gamma[:, None, None]
        diff = pos[:, None] - pos[None, :]
        if backward:
            diff = -diff
        inner = jnp.where(diff[None] >= 0, jnp.exp(lg * jnp.maximum(diff, 0.0)[None]), 0.0)
        cross_pos = (c - pos) if backward else (pos + 1.0)
        into_pos = pos if backward else (c - 1.0 - pos)
        cross = jnp.exp(log_gamma[:, None] * cross_pos[None])[:, :, None] * ones
        into = jnp.exp(log_gamma[:, None] * into_pos[None])[:, :, None] * ones
        return jnp.stack([inner, cross, into], axis=1), jnp.exp(log_gamma * c)

    lg_f = jnp.log1p(-jnp.exp(decay_f.astype(F32)))
    lg_b = jnp.log1p(-jnp.exp(decay_b.astype(F32)))
    tab_f, dec_f = tables(lg_f, False)
    tab_b, dec_b = tables(lg_b, True)
    return jnp.stack([tab_f, tab_b]), jnp.stack([dec_f, dec_b])


def _rotary_tables():
    d = RET_DK
    inv = ROPE_BASE ** (-jnp.arange(0, d, 2, dtype=F32) / d)
    ang = jnp.arange(SEQ, dtype=F32)[:, None] * inv[None, :]
    return jnp.cos(ang), jnp.sin(ang)


NA_QROWS = 4
NA_GROUPS = GRID_ROWS // NA_QROWS
NA_SLAB_ROWS = NA_QROWS + NA_KH
NA_QTOK = NA_QROWS * GRID_W
NA_SLAB = NA_SLAB_ROWS * GRID_W


def _na_kernel(q_ref, k_ref, v_ref, bias_ref, o_ref):
    scale = NA_HEAD_DIM ** -0.5

    def group(g, carry):
        s0 = jnp.clip(NA_QROWS * g - NA_KH // 2, 0, GRID_ROWS - NA_SLAB_ROWS)
        kind = jnp.where(g == 0, 0, jnp.where(g == NA_GROUPS - 1, 2, 1))
        rows = pl.ds(pl.multiple_of(g * NA_QTOK, NA_QTOK), NA_QTOK)
        slab = pl.ds(pl.multiple_of(s0 * GRID_W, NA_QTOK), NA_SLAB)
        s = lax.dot_general(q_ref[rows, :], k_ref[slab, :], (((1,), (1,)), ((), ())),
                            preferred_element_type=F32)
        s = s * scale + bias_ref[kind]
        e = jnp.exp(s - jnp.max(s, axis=-1, keepdims=True))
        p = e * (1.0 / jnp.sum(e, axis=-1, keepdims=True))
        o = jnp.dot(p.astype(BF16), v_ref[slab, :], preferred_element_type=F32)
        o_ref[rows, :] = o.astype(BF16)
        return carry

    lax.fori_loop(0, NA_GROUPS, group, 0, unroll=4)


def _na_bias_tables(rpb):
    pad = GRID_W - NA_KW
    padded = jnp.pad(rpb.astype(F32), ((0, 0), (0, 0), (pad, pad)))
    toe = jnp.stack([padded[:, :, GRID_W - 1 - c: 2 * GRID_W - 1 - c] for c in range(GRID_W)], axis=2)
    cq = np.arange(GRID_W)
    ck = np.arange(GRID_W)
    qwin = np.clip(cq - NA_KW // 2, 0, GRID_W - NA_KW)
    col_ok = (ck[None, :] >= qwin[:, None]) & (ck[None, :] < qwin[:, None] + NA_KW)
    toe = jnp.where(col_ok[None, None], toe, MASK_VALUE)
    masked = jnp.full((NA_HEADS, GRID_W, GRID_W), MASK_VALUE, F32)

    geometries = (
        lambda rq, rk: (rk < NA_KH, rk - rq + NA_KH - 1),
        lambda rq, rk: (rq <= rk < rq + NA_KH, rk - rq + NA_KH // 2 - 1),
        lambda rq, rk: (rk >= NA_QROWS, rk - rq - 1),
    )
    tables = []
    for geometry in geometries:
        rows = []
        for rq in range(NA_QROWS):
            blocks = []
            for rk in range(NA_SLAB_ROWS):
                ok, dr = geometry(rq, rk)
                blocks.append(toe[:, dr] if ok else masked)
            rows.append(jnp.concatenate(blocks, axis=2))
        tables.append(jnp.concatenate(rows, axis=1))
    return jnp.stack(tables, axis=1)


def _neighbourhood_attention(qkv, bias):
    t = qkv.shape[0]
    batch = t // SEQ
    hd = NA_HEAD_DIM
    return pl.pallas_call(
        _na_kernel,
        grid=(NA_HEADS, batch),
        in_specs=[
            pl.BlockSpec((SEQ, hd), lambda h, b: (b, h)),
            pl.BlockSpec((SEQ, hd), lambda h, b: (b, NA_HEADS + h)),
            pl.BlockSpec((SEQ, hd), lambda h, b: (b, 2 * NA_HEADS + h)),
            pl.BlockSpec((None, 3, NA_QTOK, NA_SLAB), lambda h, b: (h, 0, 0, 0)),
        ],
        out_specs=pl.BlockSpec((SEQ, hd), lambda h, b: (b, h)),
        out_shape=jax.ShapeDtypeStruct((t, D_MODEL), BF16),
        compiler_params=_params(("arbitrary", "arbitrary")),
        name="neighbourhood_attention",
    )(qkv, qkv, qkv, bias)


def _trunk(x, p):
    for i in range(DEPTH):
        x = _ffn(x, p["norm_ffn1"][i], p["ffn1_w_in"][i], p["ffn1_w_out"][i])
        j = i // 2
        if i % 2 == 0:
            qk = _proj(x, p["norm_mix"][i], p["ret_w_qk"][j], rotary=p["rotary"])
            vg = _proj(x, p["norm_mix"][i], p["ret_w_vg"][j])
            tabs, dec = p["ret_tables"][j]
            mixed = _retention_scan(qk, vg, tabs, dec, p["ret_gn_f"][j], p["ret_gn_b"][j])
            x = _out_proj(mixed, p["ret_w_out"][j], x)
        else:
            qkv = _proj(x, p["norm_mix"][i], p["na_w_in"][j])
            att = _neighbourhood_attention(qkv, p["na_bias"][j])
            x = _out_proj(att, p["na_w_out"][j], x)
        x = _ffn(x, p["norm_ffn2"][i], p["ffn2_w_in"][i], p["ffn2_w_out"][i],
                 final_gain=p["norm_final"] if i == DEPTH - 1 else None)
    return x


def kernel(x_prompt, x_sample, norm_ffn1, ffn1_w_in, ffn1_w_out, norm_mix, norm_ffn2, ffn2_w_in, ffn2_w_out,
           ret_w_in, ret_w_out, ret_decay_f, ret_decay_b, ret_gn_f, ret_gn_b, na_w_in, na_w_out, na_rpb,
           norm_final):
    bf = lambda w: w.astype(BF16)
    n_ret = ret_w_in.shape[0]
    n_na = na_w_in.shape[0]
    p = dict(
        norm_ffn1=norm_ffn1, norm_mix=norm_mix, norm_ffn2=norm_ffn2, norm_final=norm_final,
        ffn1_w_in=bf(ffn1_w_in), ffn1_w_out=bf(ffn1_w_out), ffn2_w_in=bf(ffn2_w_in), ffn2_w_out=bf(ffn2_w_out),
        ret_w_qk=bf(ret_w_in[:, :, :2 * D_MODEL]), ret_w_vg=bf(ret_w_in[:, :, 2 * D_MODEL:]),
        ret_w_out=bf(ret_w_out), ret_gn_f=ret_gn_f, ret_gn_b=ret_gn_b,
        ret_tables=[_retention_tables(ret_decay_f[j], ret_decay_b[j]) for j in range(n_ret)],
        rotary=_rotary_tables(),
        na_w_in=bf(na_w_in), na_w_out=bf(na_w_out),
        na_bias=[_na_bias_tables(na_rpb[j]) for j in range(n_na)],
    )
    outs = []
    for x in (x_prompt, x_sample):
        y = _trunk(x.reshape(-1, D_MODEL), p)
        outs.append(y.reshape(x.shape))
    return tuple(outs)
```

```python
import functools

import numpy as np
import jax
import jax.numpy as jnp
from jax import lax
from jax.experimental import pallas as pl
from jax.experimental.pallas import tpu as pltpu

D_MODEL = 2048
SEQ = 4096
DEPTH = 2
GRID_W = 64
D_FF = 5632
RET_HEADS = 8
RET_DK = D_MODEL // RET_HEADS
RET_DV = 2 * RET_DK
RET_VDIM = RET_HEADS * RET_DV
ROPE_BASE = 10000.0
NA_HEADS = 16
NA_HEAD_DIM = D_MODEL // NA_HEADS
NA_KH = 8
NA_KW = 16
RMS_EPS = 1e-6
MASK_VALUE = -1e30

F32 = jnp.float32
BF16 = jnp.bfloat16

V7X_VMEM_BYTES = 64 * 1024 * 1024
VMEM_LIMIT_BYTES = V7X_VMEM_BYTES - 8 * 1024 * 1024

FFN_TM, FFN_TF = 512, 512
PROJ_TM, PROJ_TN = 1024, 1024
OUT_TM = 512
SCAN_TC = 1024
SCAN_CHUNK = 256
GRID_ROWS = SEQ // GRID_W


def _params(semantics):
    return pltpu.CompilerParams(dimension_semantics=semantics, vmem_limit_bytes=VMEM_LIMIT_BYTES)


def _rmsnorm(xf, g):
    ms = jnp.mean(xf * xf, axis=-1, keepdims=True)
    return xf * lax.rsqrt(ms + RMS_EPS) * g


def _silu(g):
    return g / (1.0 + jnp.exp(-g))


FFN_NORM_ROWS = 64
FFN_NORM_STEPS = FFN_TM // FFN_NORM_ROWS


def _ffn_kernel(x_ref, xn_ref, g_ref, wg_ref, wu_ref, wo_ref, *rest, final_norm):
    if final_norm:
        gfin_ref, o_ref, h_ref = rest
    else:
        o_ref, h_ref = rest
    i = pl.program_id(0)
    j = pl.program_id(1)
    slot = i % 2

    @pl.when((i == 0) & (j == 0))
    def _():
        h_ref[0] = _rmsnorm(x_ref[...], g_ref[...]).astype(BF16)

    @pl.when(j == 0)
    def _():
        o_ref[...] = x_ref[...]

    h = h_ref[slot]
    g = jnp.dot(h, wg_ref[...], preferred_element_type=F32)
    u = jnp.dot(h, wu_ref[...], preferred_element_type=F32)
    a = (_silu(g) * u * 0.5).astype(BF16)
    o_ref[...] += jnp.dot(a, wo_ref[...], preferred_element_type=F32)

    rows = pl.ds(pl.multiple_of(jnp.minimum(j, FFN_NORM_STEPS - 1) * FFN_NORM_ROWS, FFN_NORM_ROWS),
                 FFN_NORM_ROWS)
    h_ref[1 - slot, rows, :] = _rmsnorm(xn_ref[rows, :], g_ref[...]).astype(BF16)

    if final_norm:
        @pl.when(j == pl.num_programs(1) - 1)
        def _():
            o_ref[...] = _rmsnorm(o_ref[...], gfin_ref[...])


def _ffn(x, gain, w_in, w_out, final_gain=None):
    t = x.shape[0]
    nj = D_FF // FFN_TF
    ni = t // FFN_TM
    assert FFN_NORM_STEPS <= nj
    in_specs = [
        pl.BlockSpec((FFN_TM, D_MODEL), lambda i, j: (i, 0)),
        pl.BlockSpec((FFN_TM, D_MODEL), lambda i, j: (jnp.minimum(i + 1, ni - 1), 0)),
        pl.BlockSpec((1, D_MODEL), lambda i, j: (0, 0)),
        pl.BlockSpec((D_MODEL, FFN_TF), lambda i, j: (0, j)),
        pl.BlockSpec((D_MODEL, FFN_TF), lambda i, j: (0, nj + j)),
        pl.BlockSpec((FFN_TF, D_MODEL), lambda i, j: (j, 0)),
    ]
    args = [x, x, gain.reshape(1, D_MODEL), w_in, w_in, w_out]
    if final_gain is not None:
        in_specs.append(pl.BlockSpec((1, D_MODEL), lambda i, j: (0, 0)))
        args.append(final_gain.reshape(1, D_MODEL))
    return pl.pallas_call(
        functools.partial(_ffn_kernel, final_norm=final_gain is not None),
        grid=(ni, nj),
        in_specs=in_specs,
        out_specs=pl.BlockSpec((FFN_TM, D_MODEL), lambda i, j: (i, 0)),
        out_shape=jax.ShapeDtypeStruct((t, D_MODEL), F32),
        scratch_shapes=[pltpu.VMEM((2, FFN_TM, D_MODEL), BF16)],
        compiler_params=_params(("arbitrary", "arbitrary")),
        name="swiglu_ffn",
    )(*args)


def _proj_kernel(x_ref, g_ref, w_ref, *rest, rotary):
    if rotary:
        cos_ref, sin_ref, o_ref, h_ref = rest
    else:
        o_ref, h_ref = rest
    n = pl.program_id(1)

    @pl.when(n == 0)
    def _():
        h_ref[...] = _rmsnorm(x_ref[...], g_ref[...]).astype(BF16)

    y = jnp.dot(h_ref[...], w_ref[...], preferred_element_type=F32)
    if not rotary:
        o_ref[...] = y.astype(BF16)
        return
    cos = cos_ref[...]
    sin = sin_ref[...]
    scale = jnp.where(n < pl.num_programs(1) // 2, 1.0, RET_DK ** -0.5).astype(F32)
    half = RET_DK // 2
    for hh in range(PROJ_TN // RET_DK):
        x1 = y[:, hh * RET_DK: hh * RET_DK + half]
        x2 = y[:, hh * RET_DK + half: (hh + 1) * RET_DK]
        o_ref[:, hh * RET_DK: hh * RET_DK + half] = ((x1 * cos - x2 * sin) * scale).astype(BF16)
        o_ref[:, hh * RET_DK + half: (hh + 1) * RET_DK] = ((x1 * sin + x2 * cos) * scale).astype(BF16)


def _proj(x, gain, w, rotary=None):
    t = x.shape[0]
    n_out = w.shape[1]
    in_specs = [
        pl.BlockSpec((PROJ_TM, D_MODEL), lambda i, n: (i, 0)),
        pl.BlockSpec((1, D_MODEL), lambda i, n: (0, 0)),
        pl.BlockSpec((D_MODEL, PROJ_TN), lambda i, n: (0, n)),
    ]
    args = [x, gain.reshape(1, D_MODEL), w]
    if rotary is not None:
        seq_tiles = SEQ // PROJ_TM
        tab_spec = pl.BlockSpec((PROJ_TM, RET_DK // 2), lambda i, n: (i % seq_tiles, 0))
        in_specs += [tab_spec, tab_spec]
        args += list(rotary)
    return pl.pallas_call(
        functools.partial(_proj_kernel, rotary=rotary is not None),
        grid=(t // PROJ_TM, n_out // PROJ_TN),
        in_specs=in_specs,
        out_specs=pl.BlockSpec((PROJ_TM, PROJ_TN), lambda i, n: (i, n)),
        out_shape=jax.ShapeDtypeStruct((t, n_out), BF16),
        scratch_shapes=[pltpu.VMEM((PROJ_TM, D_MODEL), BF16)],
        compiler_params=_params(("parallel", "arbitrary")),
        name="norm_proj_rotary" if rotary is not None else "norm_proj",
    )(*args)


def _out_kernel(a_ref, w_ref, x_ref, o_ref):
    o_ref[...] = x_ref[...] + jnp.dot(a_ref[...], w_ref[...], preferred_element_type=F32)


def _out_proj(a, w, x):
    t, k = a.shape
    return pl.pallas_call(
        _out_kernel,
        grid=(t // OUT_TM,),
        in_specs=[
            pl.BlockSpec((OUT_TM, k), lambda i: (i, 0)),
            pl.BlockSpec((k, D_MODEL), lambda i: (0, 0), pipeline_mode=pl.Buffered(1)),
            pl.BlockSpec((OUT_TM, D_MODEL), lambda i: (i, 0)),
        ],
        out_specs=pl.BlockSpec((OUT_TM, D_MODEL), lambda i: (i, 0)),
        out_shape=jax.ShapeDtypeStruct((t, D_MODEL), F32),
        compiler_params=_params(("parallel",)),
        name="out_proj_residual",
    )(a, w, x)


SCAN_NBLK = SEQ // SCAN_TC
SCAN_CHUNKS = SCAN_TC // SCAN_CHUNK


def _scan_kernel(dec_ref, q_ref, k_ref, v_ref, gf_ref, gb_ref, tab_ref, gnf_ref, gnb_ref,
                 o_ref, r_ref, ob_ref):
    h = pl.program_id(1)
    t = pl.program_id(2)
    c_len = SCAN_CHUNK

    @pl.when((t == 0) | (t == SCAN_NBLK))
    def _():
        r_ref[...] = jnp.zeros_like(r_ref)

    def run(backward):
        inner = tab_ref[0]
        cross = jnp.concatenate([tab_ref[1]] * (RET_DV // c_len), axis=1)
        into = jnp.concatenate([tab_ref[2]] * (RET_DK // c_len), axis=1)
        dec = dec_ref[1 if backward else 0, h]
        gate_ref = gb_ref if backward else gf_ref
        gn = (gnb_ref if backward else gnf_ref)[...]
        blk = (SCAN_NBLK - 1 - t) if backward else (t - SCAN_NBLK)
        order = range(SCAN_CHUNKS - 1, -1, -1) if backward else range(SCAN_CHUNKS)
        for c in order:
            rows = pl.ds(c * c_len, c_len)
            q = q_ref[rows, :]
            k = k_ref[rows, :]
            v = v_ref[rows, :]
            s = lax.dot_general(q, k, (((1,), (1,)), ((), ())), preferred_element_type=F32) * inner
            kd = (k.astype(F32) * into).astype(BF16)
            r = r_ref[...]
            sv = jnp.dot(jnp.concatenate([s.astype(BF16), kd.T], axis=0), v, preferred_element_type=F32)
            o = sv[:c_len] + jnp.dot(q, r.astype(BF16), preferred_element_type=F32) * cross
            r_ref[...] = r * dec + sv[c_len:]
            on = o * lax.rsqrt(jnp.mean(o * o, axis=-1, keepdims=True) + RMS_EPS) * gn
            m = _silu(gate_ref[rows, :].astype(F32)) * on
            ob_rows = pl.ds(pl.multiple_of(blk * SCAN_TC + c * c_len, c_len), c_len)
            if backward:
                ob_ref[ob_rows, :] = m
            else:
                o_ref[rows, :] = (m + ob_ref[ob_rows, :]).astype(BF16)

    @pl.when(t < SCAN_NBLK)
    def _():
        o_ref[...] = jnp.zeros_like(o_ref)
        run(True)

    @pl.when(t >= SCAN_NBLK)
    def _():
        run(False)


def _retention_scan(qk, vg, tabs, dec, gn_f, gn_b):
    t = qk.shape[0]
    batch = t // SEQ
    nb = SCAN_NBLK

    def bwd_blk(s):
        return jnp.where(s < nb, nb - 1 - s, 0)

    def fwd_blk(s):
        return jnp.where(s < nb, 0, s - nb)

    def cur_blk(s):
        return jnp.where(s < nb, nb - 1 - s, s - nb)

    nh = RET_HEADS
    in_specs = [
        pl.BlockSpec(memory_space=pltpu.SMEM),
        pl.BlockSpec((SCAN_TC, RET_DK), lambda b, h, s: (b * nb + cur_blk(s), h)),
        pl.BlockSpec((SCAN_TC, RET_DK), lambda b, h, s: (b * nb + cur_blk(s), nh + h)),
        pl.BlockSpec((SCAN_TC, RET_DV), lambda b, h, s: (b * nb + cur_blk(s), h)),
        pl.BlockSpec((SCAN_TC, RET_DV), lambda b, h, s: (b * nb + fwd_blk(s), nh + h)),
        pl.BlockSpec((SCAN_TC, RET_DV), lambda b, h, s: (b * nb + bwd_blk(s), 2 * nh + h)),
        pl.BlockSpec((None, None, 3, SCAN_CHUNK, SCAN_CHUNK),
                     lambda b, h, s: (jnp.where(s < nb, 1, 0), h, 0, 0, 0)),
        pl.BlockSpec((1, RET_DV), lambda b, h, s: (0, h)),
        pl.BlockSpec((1, RET_DV), lambda b, h, s: (0, h)),
    ]
    return pl.pallas_call(
        _scan_kernel,
        grid=(batch, RET_HEADS, 2 * nb),
        in_specs=in_specs,
        out_specs=pl.BlockSpec((SCAN_TC, RET_DV), lambda b, h, s: (b * nb + fwd_blk(s), h)),
        out_shape=jax.ShapeDtypeStruct((t, RET_VDIM), BF16),
        scratch_shapes=[pltpu.VMEM((RET_DK, RET_DV), F32), pltpu.VMEM((SEQ, RET_DV), F32)],
        compiler_params=_params(("arbitrary", "arbitrary", "arbitrary")),
        name="retention_scan",
    )(dec, qk, qk, vg, vg, vg, tabs, gn_f.reshape(1, RET_VDIM), gn_b.reshape(1, RET_VDIM))


def _retention_tables(decay_f, decay_b):
    c = SCAN_CHUNK
    pos = jnp.arange(c, dtype=F32)
    ones = jnp.ones((1, 1, c), F32)

    def tables(log_gamma, backward):
        lg = log_gamma[:, None, None]
        diff = pos[:, None] - pos[None, :]
        if backward:
            diff = -diff
        inner = jnp.where(diff[None] >= 0, jnp.exp(lg * jnp.maximum(diff, 0.0)[None]), 0.0)
        cross_pos = (c - pos) if backward else (pos + 1.0)
        into_pos = pos if backward else (c - 1.0 - pos)
        cross = jnp.exp(log_gamma[:, None] * cross_pos[None])[:, :, None] * ones
        into = jnp.exp(log_gamma[:, None] * into_pos[None])[:, :, None] * ones
        return jnp.stack([inner, cross, into], axis=1), jnp.exp(log_gamma * c)

    lg_f = jnp.log1p(-jnp.exp(decay_f.astype(F32)))
    lg_b = jnp.log1p(-jnp.exp(decay_b.astype(F32)))
    tab_f, dec_f = tables(lg_f, False)
    tab_b, dec_b = tables(lg_b, True)
    return jnp.stack([tab_f, tab_b]), jnp.stack([dec_f, dec_b])


def _rotary_tables():
    d = RET_DK
    inv = ROPE_BASE ** (-jnp.arange(0, d, 2, dtype=F32) / d)
    ang = jnp.arange(SEQ, dtype=F32)[:, None] * inv[None, :]
    return jnp.cos(ang), jnp.sin(ang)


NA_QROWS = 4
NA_GROUPS = GRID_ROWS // NA_QROWS
NA_SLAB_ROWS = NA_QROWS + NA_KH
NA_QTOK = NA_QROWS * GRID_W
NA_SLAB = NA_SLAB_ROWS * GRID_W


def _na_kernel(q_ref, k_ref, v_ref, bias_ref, o_ref):
    scale = NA_HEAD_DIM ** -0.5

    def group(g, carry):
        s0 = jnp.clip(NA_QROWS * g - NA_KH // 2, 0, GRID_ROWS - NA_SLAB_ROWS)
        kind = jnp.where(g == 0, 0, jnp.where(g == NA_GROUPS - 1, 2, 1))
        rows = pl.ds(pl.multiple_of(g * NA_QTOK, NA_QTOK), NA_QTOK)
        slab = pl.ds(pl.multiple_of(s0 * GRID_W, NA_QTOK), NA_SLAB)
        s = lax.dot_general(q_ref[rows, :], k_ref[slab, :], (((1,), (1,)), ((), ())),
                            preferred_element_type=F32)
        s = s * scale + bias_ref[kind]
        e = jnp.exp(s - jnp.max(s, axis=-1, keepdims=True))
        p = e * (1.0 / jnp.sum(e, axis=-1, keepdims=True))
        o = jnp.dot(p.astype(BF16), v_ref[slab, :], preferred_element_type=F32)
        o_ref[rows, :] = o.astype(BF16)
        return carry

    lax.fori_loop(0, NA_GROUPS, group, 0, unroll=4)


def _na_bias_tables(rpb):
    pad = GRID_W - NA_KW
    padded = jnp.pad(rpb.astype(F32), ((0, 0), (0, 0), (pad, pad)))
    toe = jnp.stack([padded[:, :, GRID_W - 1 - c: 2 * GRID_W - 1 - c] for c in range(GRID_W)], axis=2)
    cq = np.arange(GRID_W)
    ck = np.arange(GRID_W)
    qwin = np.clip(cq - NA_KW // 2, 0, GRID_W - NA_KW)
    col_ok = (ck[None, :] >= qwin[:, None]) & (ck[None, :] < qwin[:, None] + NA_KW)
    toe = jnp.where(col_ok[None, None], toe, MASK_VALUE)
    masked = jnp.full((NA_HEADS, GRID_W, GRID_W), MASK_VALUE, F32)

    geometries = (
        lambda rq, rk: (rk < NA_KH, rk - rq + NA_KH - 1),
        lambda rq, rk: (rq <= rk < rq + NA_KH, rk - rq + NA_KH // 2 - 1),
        lambda rq, rk: (rk >= NA_QROWS, rk - rq - 1),
    )
    tables = []
    for geometry in geometries:
        rows = []
        for rq in range(NA_QROWS):
            blocks = []
            for rk in range(NA_SLAB_ROWS):
                ok, dr = geometry(rq, rk)
                blocks.append(toe[:, dr] if ok else masked)
            rows.append(jnp.concatenate(blocks, axis=2))
        tables.append(jnp.concatenate(rows, axis=1))
    return jnp.stack(tables, axis=1)


def _neighbourhood_attention(qkv, bias):
    t = qkv.shape[0]
    batch = t // SEQ
    hd = NA_HEAD_DIM
    return pl.pallas_call(
        _na_kernel,
        grid=(NA_HEADS, batch),
        in_specs=[
            pl.BlockSpec((SEQ, hd), lambda h, b: (b, h)),
            pl.BlockSpec((SEQ, hd), lambda h, b: (b, NA_HEADS + h)),
            pl.BlockSpec((SEQ, hd), lambda h, b: (b, 2 * NA_HEADS + h)),
            pl.BlockSpec((None, 3, NA_QTOK, NA_SLAB), lambda h, b: (h, 0, 0, 0)),
        ],
        out_specs=pl.BlockSpec((SEQ, hd), lambda h, b: (b, h)),
        out_shape=jax.ShapeDtypeStruct((t, D_MODEL), BF16),
        compiler_params=_params(("arbitrary", "arbitrary")),
        name="neighbourhood_attention",
    )(qkv, qkv, qkv, bias)


def _trunk(x, p):
    for i in range(DEPTH):
        x = _ffn(x, p["norm_ffn1"][i], p["ffn1_w_in"][i], p["ffn1_w_out"][i])
        j = i // 2
        if i % 2 == 0:
            qk = _proj(x, p["norm_mix"][i], p["ret_w_qk"][j], rotary=p["rotary"])
            vg = _proj(x, p["norm_mix"][i], p["ret_w_vg"][j])
            tabs, dec = p["ret_tables"][j]
            mixed = _retention_scan(qk, vg, tabs, dec, p["ret_gn_f"][j], p["ret_gn_b"][j])
            x = _out_proj(mixed, p["ret_w_out"][j], x)
        else:
            qkv = _proj(x, p["norm_mix"][i], p["na_w_in"][j])
            att = _neighbourhood_attention(qkv, p["na_bias"][j])
            x = _out_proj(att, p["na_w_out"][j], x)
        x = _ffn(x, p["norm_ffn2"][i], p["ffn2_w_in"][i], p["ffn2_w_out"][i],
                 final_gain=p["norm_final"] if i == DEPTH - 1 else None)
    return x


def kernel(x_prompt, x_sample, norm_ffn1, ffn1_w_in, ffn1_w_out, norm_mix, norm_ffn2, ffn2_w_in, ffn2_w_out,
           ret_w_in, ret_w_out, ret_decay_f, ret_decay_b, ret_gn_f, ret_gn_b, na_w_in, na_w_out, na_rpb,
           norm_final):
    n_ret = ret_w_in.shape[0]
    n_na = na_w_in.shape[0]

    def bf(w, n, cols=slice(None)):
        return [w[i][:, cols].astype(BF16) for i in range(n)]

    p = dict(
        norm_ffn1=norm_ffn1, norm_mix=norm_mix, norm_ffn2=norm_ffn2, norm_final=norm_final,
        ffn1_w_in=bf(ffn1_w_in, DEPTH), ffn1_w_out=bf(ffn1_w_out, DEPTH),
        ffn2_w_in=bf(ffn2_w_in, DEPTH), ffn2_w_out=bf(ffn2_w_out, DEPTH),
        ret_w_qk=bf(ret_w_in, n_ret, slice(0, 2 * D_MODEL)),
        ret_w_vg=bf(ret_w_in, n_ret, slice(2 * D_MODEL, None)),
        ret_w_out=bf(ret_w_out, n_ret), ret_gn_f=ret_gn_f, ret_gn_b=ret_gn_b,
        ret_tables=[_retention_tables(ret_decay_f[j], ret_decay_b[j]) for j in range(n_ret)],
        rotary=_rotary_tables(),
        na_w_in=bf(na_w_in, n_na), na_w_out=bf(na_w_out, n_na),
        na_bias=[_na_bias_tables(na_rpb[j]) for j in range(n_na)],
    )
    outs = []
    for x in (x_prompt, x_sample):
        y = _trunk(x.reshape(-1, D_MODEL), p)
        outs.append(y.reshape(x.shape))
    return tuple(outs)
```

```python
import functools

import numpy as np
import jax
import jax.numpy as jnp
from jax import lax
from jax.experimental import pallas as pl
from jax.experimental.pallas import tpu as pltpu

D_MODEL = 2048
SEQ = 4096
DEPTH = 2
GRID_W = 64
D_FF = 5632
RET_HEADS = 8
RET_DK = D_MODEL // RET_HEADS
RET_DV = 2 * RET_DK
RET_VDIM = RET_HEADS * RET_DV
RET_IN = 2 * D_MODEL + 3 * RET_VDIM
ROPE_BASE = 10000.0
NA_HEADS = 16
NA_HEAD_DIM = D_MODEL // NA_HEADS
NA_KH = 8
NA_KW = 16
RMS_EPS = 1e-6
MASK_VALUE = -1e30

F32 = jnp.float32
BF16 = jnp.bfloat16

V7X_VMEM_BYTES = 64 * 1024 * 1024
VMEM_LIMIT_BYTES = V7X_VMEM_BYTES - 8 * 1024 * 1024

FFN_TM, FFN_TF = 1024, 512
PROJ_TM, PROJ_TN = 1024, 1024
OUT_TM = 512
SCAN_TC = 1024
SCAN_CHUNK = 256
GRID_ROWS = SEQ // GRID_W


def _params(semantics):
    return pltpu.CompilerParams(dimension_semantics=semantics, vmem_limit_bytes=VMEM_LIMIT_BYTES)


def _rmsnorm(xf, g):
    ms = jnp.mean(xf * xf, axis=-1, keepdims=True)
    return xf * lax.rsqrt(ms + RMS_EPS) * g


def _silu(g):
    return g / (1.0 + jnp.exp(-g))


def _ffn_kernel(x_ref, g_ref, wg_ref, wu_ref, wo_ref, *rest, final_norm):
    if final_norm:
        gfin_ref, o_ref, h_ref = rest
    else:
        o_ref, h_ref = rest
    j = pl.program_id(1)

    @pl.when(j == 0)
    def _():
        x = x_ref[...]
        h_ref[...] = _rmsnorm(x, g_ref[...]).astype(BF16)
        o_ref[...] = x

    h = h_ref[...]
    g = jnp.dot(h, wg_ref[...], preferred_element_type=F32)
    u = jnp.dot(h, wu_ref[...], preferred_element_type=F32)
    a = (_silu(g) * u * 0.5).astype(BF16)
    o_ref[...] += jnp.dot(a, wo_ref[...], preferred_element_type=F32)

    if final_norm:
        @pl.when(j == pl.num_programs(1) - 1)
        def _():
            o_ref[...] = _rmsnorm(o_ref[...], gfin_ref[...])


def _ffn(x, gain, w_in, w_out, layer, final_gain=None):
    t = x.shape[0]
    nj = D_FF // FFN_TF
    in_specs = [
        pl.BlockSpec((FFN_TM, D_MODEL), lambda i, j: (i, 0)),
        pl.BlockSpec((1, D_MODEL), lambda i, j: (0, 0)),
        pl.BlockSpec((None, D_MODEL, FFN_TF), lambda i, j: (layer, 0, j)),
        pl.BlockSpec((None, D_MODEL, FFN_TF), lambda i, j: (layer, 0, nj + j)),
        pl.BlockSpec((None, FFN_TF, D_MODEL), lambda i, j: (layer, j, 0)),
    ]
    args = [x, gain.reshape(1, D_MODEL), w_in, w_in, w_out]
    if final_gain is not None:
        in_specs.append(pl.BlockSpec((1, D_MODEL), lambda i, j: (0, 0)))
        args.append(final_gain.reshape(1, D_MODEL))
    return pl.pallas_call(
        functools.partial(_ffn_kernel, final_norm=final_gain is not None),
        grid=(t // FFN_TM, nj),
        in_specs=in_specs,
        out_specs=pl.BlockSpec((FFN_TM, D_MODEL), lambda i, j: (i, 0)),
        out_shape=jax.ShapeDtypeStruct((t, D_MODEL), F32),
        scratch_shapes=[pltpu.VMEM((FFN_TM, D_MODEL), BF16)],
        compiler_params=_params(("parallel", "arbitrary")),
        name="swiglu_ffn",
    )(*args)


def _proj_kernel(x_ref, g_ref, w_ref, *rest, rotary):
    if rotary:
        cos_ref, sin_ref, o_ref, h_ref = rest
    else:
        o_ref, h_ref = rest
    n = pl.program_id(1)

    @pl.when(n == 0)
    def _():
        h_ref[...] = _rmsnorm(x_ref[...], g_ref[...]).astype(BF16)

    y = jnp.dot(h_ref[...], w_ref[...], preferred_element_type=F32)
    if not rotary:
        o_ref[...] = y.astype(BF16)
        return
    cos = cos_ref[...]
    sin = sin_ref[...]
    scale = jnp.where(n < pl.num_programs(1) // 2, 1.0, RET_DK ** -0.5).astype(F32)
    half = RET_DK // 2
    for hh in range(PROJ_TN // RET_DK):
        x1 = y[:, hh * RET_DK: hh * RET_DK + half]
        x2 = y[:, hh * RET_DK + half: (hh + 1) * RET_DK]
        o_ref[:, hh * RET_DK: hh * RET_DK + half] = ((x1 * cos - x2 * sin) * scale).astype(BF16)
        o_ref[:, hh * RET_DK + half: (hh + 1) * RET_DK] = ((x1 * sin + x2 * cos) * scale).astype(BF16)


def _proj(x, gain, w, layer, cols, rotary=None):
    t = x.shape[0]
    n_out = cols[1] - cols[0]
    col0 = cols[0] // PROJ_TN
    in_specs = [
        pl.BlockSpec((PROJ_TM, D_MODEL), lambda i, n: (i, 0)),
        pl.BlockSpec((1, D_MODEL), lambda i, n: (0, 0)),
        pl.BlockSpec((None, D_MODEL, PROJ_TN), lambda i, n: (layer, 0, col0 + n)),
    ]
    args = [x, gain.reshape(1, D_MODEL), w]
    if rotary is not None:
        seq_tiles = SEQ // PROJ_TM
        tab_spec = pl.BlockSpec((PROJ_TM, RET_DK // 2), lambda i, n: (i % seq_tiles, 0))
        in_specs += [tab_spec, tab_spec]
        args += list(rotary)
    return pl.pallas_call(
        functools.partial(_proj_kernel, rotary=rotary is not None),
        grid=(t // PROJ_TM, n_out // PROJ_TN),
        in_specs=in_specs,
        out_specs=pl.BlockSpec((PROJ_TM, PROJ_TN), lambda i, n: (i, n)),
        out_shape=jax.ShapeDtypeStruct((t, n_out), BF16),
        scratch_shapes=[pltpu.VMEM((PROJ_TM, D_MODEL), BF16)],
        compiler_params=_params(("parallel", "arbitrary")),
        name="norm_proj_rotary" if rotary is not None else "norm_proj",
    )(*args)


def _out_kernel(a_ref, w_ref, x_ref, o_ref):
    o_ref[...] = x_ref[...] + jnp.dot(a_ref[...], w_ref[...], preferred_element_type=F32)


def _out_proj(a, w, layer, x):
    t, k = a.shape
    return pl.pallas_call(
        _out_kernel,
        grid=(t // OUT_TM,),
        in_specs=[
            pl.BlockSpec((OUT_TM, k), lambda i: (i, 0)),
            pl.BlockSpec((None, k, D_MODEL), lambda i: (layer, 0, 0), pipeline_mode=pl.Buffered(1)),
            pl.BlockSpec((OUT_TM, D_MODEL), lambda i: (i, 0)),
        ],
        out_specs=pl.BlockSpec((OUT_TM, D_MODEL), lambda i: (i, 0)),
        out_shape=jax.ShapeDtypeStruct((t, D_MODEL), F32),
        compiler_params=_params(("parallel",)),
        name="out_proj_residual",
    )(a, w, x)


SCAN_NBLK = SEQ // SCAN_TC
SCAN_CHUNKS = SCAN_TC // SCAN_CHUNK


def _scan_kernel(dec_ref, q_ref, k_ref, v_ref, gf_ref, gb_ref, tab_ref, gnf_ref, gnb_ref,
                 o_ref, r_ref, ob_ref):
    h = pl.program_id(1)
    t = pl.program_id(2)
    c_len = SCAN_CHUNK

    @pl.when((t == 0) | (t == SCAN_NBLK))
    def _():
        r_ref[...] = jnp.zeros_like(r_ref)

    def run(backward):
        inner = tab_ref[0]
        cross = jnp.concatenate([tab_ref[1]] * (RET_DV // c_len), axis=1)
        into = jnp.concatenate([tab_ref[2]] * (RET_DK // c_len), axis=1)
        dec = dec_ref[1 if backward else 0, h]
        gate_ref = gb_ref if backward else gf_ref
        gn = (gnb_ref if backward else gnf_ref)[...]
        blk = (SCAN_NBLK - 1 - t) if backward else (t - SCAN_NBLK)
        order = range(SCAN_CHUNKS - 1, -1, -1) if backward else range(SCAN_CHUNKS)
        for c in order:
            rows = pl.ds(c * c_len, c_len)
            q = q_ref[rows, :]
            k = k_ref[rows, :]
            v = v_ref[rows, :]
            s = lax.dot_general(q, k, (((1,), (1,)), ((), ())), preferred_element_type=F32) * inner
            kd = (k.astype(F32) * into).astype(BF16)
            r = r_ref[...]
            sv = jnp.dot(jnp.concatenate([s.astype(BF16), kd.T], axis=0), v, preferred_element_type=F32)
            o = sv[:c_len] + jnp.dot(q, r.astype(BF16), preferred_element_type=F32) * cross
            r_ref[...] = r * dec + sv[c_len:]
            on = o * lax.rsqrt(jnp.mean(o * o, axis=-1, keepdims=True) + RMS_EPS) * gn
            m = _silu(gate_ref[rows, :].astype(F32)) * on
            ob_rows = pl.ds(pl.multiple_of(blk * SCAN_TC + c * c_len, c_len), c_len)
            if backward:
                ob_ref[ob_rows, :] = m
            else:
                o_ref[rows, :] = (m + ob_ref[ob_rows, :]).astype(BF16)

    @pl.when(t < SCAN_NBLK)
    def _():
        o_ref[...] = jnp.zeros_like(o_ref)
        run(True)

    @pl.when(t >= SCAN_NBLK)
    def _():
        run(False)


def _retention_scan(qk, vg, tabs, dec, gn_f, gn_b):
    t = qk.shape[0]
    batch = t // SEQ
    nb = SCAN_NBLK

    def bwd_blk(s):
        return jnp.where(s < nb, nb - 1 - s, 0)

    def fwd_blk(s):
        return jnp.where(s < nb, 0, s - nb)

    def cur_blk(s):
        return jnp.where(s < nb, nb - 1 - s, s - nb)

    nh = RET_HEADS
    in_specs = [
        pl.BlockSpec(memory_space=pltpu.SMEM),
        pl.BlockSpec((SCAN_TC, RET_DK), lambda b, h, s: (b * nb + cur_blk(s), h)),
        pl.BlockSpec((SCAN_TC, RET_DK), lambda b, h, s: (b * nb + cur_blk(s), nh + h)),
        pl.BlockSpec((SCAN_TC, RET_DV), lambda b, h, s: (b * nb + cur_blk(s), h)),
        pl.BlockSpec((SCAN_TC, RET_DV), lambda b, h, s: (b * nb + fwd_blk(s), nh + h)),
        pl.BlockSpec((SCAN_TC, RET_DV), lambda b, h, s: (b * nb + bwd_blk(s), 2 * nh + h)),
        pl.BlockSpec((None, None, 3, SCAN_CHUNK, SCAN_CHUNK),
                     lambda b, h, s: (jnp.where(s < nb, 1, 0), h, 0, 0, 0)),
        pl.BlockSpec((1, RET_DV), lambda b, h, s: (0, h)),
        pl.BlockSpec((1, RET_DV), lambda b, h, s: (0, h)),
    ]
    return pl.pallas_call(
        _scan_kernel,
        grid=(batch, RET_HEADS, 2 * nb),
        in_specs=in_specs,
        out_specs=pl.BlockSpec((SCAN_TC, RET_DV), lambda b, h, s: (b * nb + fwd_blk(s), h)),
        out_shape=jax.ShapeDtypeStruct((t, RET_VDIM), BF16),
        scratch_shapes=[pltpu.VMEM((RET_DK, RET_DV), F32), pltpu.VMEM((SEQ, RET_DV), F32)],
        compiler_params=_params(("arbitrary", "arbitrary", "arbitrary")),
        name="retention_scan",
    )(dec, qk, qk, vg, vg, vg, tabs, gn_f.reshape(1, RET_VDIM), gn_b.reshape(1, RET_VDIM))


def _retention_tables(decay_f, decay_b):
    c = SCAN_CHUNK
    pos = jnp.arange(c, dtype=F32)
    ones = jnp.ones((1, 1, c), F32)

    def tables(log_gamma, backward):
        lg = log_gamma[:, None, None]
        diff = pos[:, None] - pos[None, :]
        if backward:
            diff = -diff
        inner = jnp.where(diff[None] >= 0, jnp.exp(lg * jnp.maximum(diff, 0.0)[None]), 0.0)
        cross_pos = (c - pos) if backward else (pos + 1.0)
        into_pos = pos if backward else (c - 1.0 - pos)
        cross = jnp.exp(log_gamma[:, None] * cross_pos[None])[:, :, None] * ones
        into = jnp.exp(log_gamma[:, None] * into_pos[None])[:, :, None] * ones
        return jnp.stack([inner, cross, into], axis=1), jnp.exp(log_gamma * c)

    lg_f = jnp.log1p(-jnp.exp(decay_f.astype(F32)))
    lg_b = jnp.log1p(-jnp.exp(decay_b.astype(F32)))
    tab_f, dec_f = tables(lg_f, False)
    tab_b, dec_b = tables(lg_b, True)
    return jnp.stack([tab_f, tab_b]), jnp.stack([dec_f, dec_b])


def _rotary_tables():
    d = RET_DK
    inv = ROPE_BASE ** (-jnp.arange(0, d, 2, dtype=F32) / d)
    ang = jnp.arange(SEQ, dtype=F32)[:, None] * inv[None, :]
    return jnp.cos(ang), jnp.sin(ang)


NA_QROWS = 4
NA_GROUPS = GRID_ROWS // NA_QROWS
NA_SLAB_ROWS = NA_QROWS + NA_KH
NA_QTOK = NA_QROWS * GRID_W
NA_SLAB = NA_SLAB_ROWS * GRID_W


def _na_kernel(q_ref, k_ref, v_ref, bias_ref, o_ref):
    scale = NA_HEAD_DIM ** -0.5

    def group(g, carry):
        s0 = jnp.clip(NA_QROWS * g - NA_KH // 2, 0, GRID_ROWS - NA_SLAB_ROWS)
        kind = jnp.where(g == 0, 0, jnp.where(g == NA_GROUPS - 1, 2, 1))
        rows = pl.ds(pl.multiple_of(g * NA_QTOK, NA_QTOK), NA_QTOK)
        slab = pl.ds(pl.multiple_of(s0 * GRID_W, NA_QTOK), NA_SLAB)
        s = lax.dot_general(q_ref[rows, :], k_ref[slab, :], (((1,), (1,)), ((), ())),
                            preferred_element_type=F32)
        s = s * scale + bias_ref[kind]
        e = jnp.exp(s - jnp.max(s, axis=-1, keepdims=True))
        p = e * (1.0 / jnp.sum(e, axis=-1, keepdims=True))
        o = jnp.dot(p.astype(BF16), v_ref[slab, :], preferred_element_type=F32)
        o_ref[rows, :] = o.astype(BF16)
        return carry

    lax.fori_loop(0, NA_GROUPS, group, 0, unroll=4)


def _na_bias_tables(rpb):
    pad = GRID_W - NA_KW
    padded = jnp.pad(rpb.astype(F32), ((0, 0), (0, 0), (pad, pad)))
    toe = jnp.stack([padded[:, :, GRID_W - 1 - c: 2 * GRID_W - 1 - c] for c in range(GRID_W)], axis=2)
    cq = np.arange(GRID_W)
    ck = np.arange(GRID_W)
    qwin = np.clip(cq - NA_KW // 2, 0, GRID_W - NA_KW)
    col_ok = (ck[None, :] >= qwin[:, None]) & (ck[None, :] < qwin[:, None] + NA_KW)
    toe = jnp.where(col_ok[None, None], toe, MASK_VALUE)
    masked = jnp.full((NA_HEADS, GRID_W, GRID_W), MASK_VALUE, F32)

    geometries = (
        lambda rq, rk: (rk < NA_KH, rk - rq + NA_KH - 1),
        lambda rq, rk: (rq <= rk < rq + NA_KH, rk - rq + NA_KH // 2 - 1),
        lambda rq, rk: (rk >= NA_QROWS, rk - rq - 1),
    )
    tables = []
    for geometry in geometries:
        rows = []
        for rq in range(NA_QROWS):
            blocks = []
            for rk in range(NA_SLAB_ROWS):
                ok, dr = geometry(rq, rk)
                blocks.append(toe[:, dr] if ok else masked)
            rows.append(jnp.concatenate(blocks, axis=2))
        tables.append(jnp.concatenate(rows, axis=1))
    return jnp.stack(tables, axis=1)


def _neighbourhood_attention(qkv, bias):
    t = qkv.shape[0]
    batch = t // SEQ
    hd = NA_HEAD_DIM
    return pl.pallas_call(
        _na_kernel,
        grid=(NA_HEADS, batch),
        in_specs=[
            pl.BlockSpec((SEQ, hd), lambda h, b: (b, h)),
            pl.BlockSpec((SEQ, hd), lambda h, b: (b, NA_HEADS + h)),
            pl.BlockSpec((SEQ, hd), lambda h, b: (b, 2 * NA_HEADS + h)),
            pl.BlockSpec((None, 3, NA_QTOK, NA_SLAB), lambda h, b: (h, 0, 0, 0)),
        ],
        out_specs=pl.BlockSpec((SEQ, hd), lambda h, b: (b, h)),
        out_shape=jax.ShapeDtypeStruct((t, D_MODEL), BF16),
        compiler_params=_params(("arbitrary", "arbitrary")),
        name="neighbourhood_attention",
    )(qkv, qkv, qkv, bias)


def _trunk(x, p):
    for i in range(DEPTH):
        x = _ffn(x, p["norm_ffn1"][i], p["ffn1_w_in"], p["ffn1_w_out"], i)
        j = i // 2
        if i % 2 == 0:
            qk = _proj(x, p["norm_mix"][i], p["ret_w_in"], j, (0, 2 * D_MODEL), rotary=p["rotary"])
            vg = _proj(x, p["norm_mix"][i], p["ret_w_in"], j, (2 * D_MODEL, RET_IN))
            tabs, dec = p["ret_tables"][j]
            mixed = _retention_scan(qk, vg, tabs, dec, p["ret_gn_f"][j], p["ret_gn_b"][j])
            x = _out_proj(mixed, p["ret_w_out"], j, x)
        else:
            qkv = _proj(x, p["norm_mix"][i], p["na_w_in"], j, (0, 3 * D_MODEL))
            att = _neighbourhood_attention(qkv, p["na_bias"][j])
            x = _out_proj(att, p["na_w_out"], j, x)
        x = _ffn(x, p["norm_ffn2"][i], p["ffn2_w_in"], p["ffn2_w_out"], i,
                 final_gain=p["norm_final"] if i == DEPTH - 1 else None)
    return x


def kernel(x_prompt, x_sample, norm_ffn1, ffn1_w_in, ffn1_w_out, norm_mix, norm_ffn2, ffn2_w_in, ffn2_w_out,
           ret_w_in, ret_w_out, ret_decay_f, ret_decay_b, ret_gn_f, ret_gn_b, na_w_in, na_w_out, na_rpb,
           norm_final):
    n_ret = ret_w_in.shape[0]
    n_na = na_w_in.shape[0]
    bf = lambda w: w.astype(BF16)
    p = dict(
        norm_ffn1=norm_ffn1, norm_mix=norm_mix, norm_ffn2=norm_ffn2, norm_final=norm_final,
        ffn1_w_in=bf(ffn1_w_in), ffn1_w_out=bf(ffn1_w_out), ffn2_w_in=bf(ffn2_w_in), ffn2_w_out=bf(ffn2_w_out),
        ret_w_in=bf(ret_w_in), ret_w_out=bf(ret_w_out), ret_gn_f=ret_gn_f, ret_gn_b=ret_gn_b,
        ret_tables=[_retention_tables(ret_decay_f[j], ret_decay_b[j]) for j in range(n_ret)],
        rotary=_rotary_tables(),
        na_w_in=bf(na_w_in), na_w_out=bf(na_w_out),
        na_bias=[_na_bias_tables(na_rpb[j]) for j in range(n_na)],
    )
    outs = []
    for x in (x_prompt, x_sample):
        y = _trunk(x.reshape(-1, D_MODEL), p)
        outs.append(y.reshape(x.shape))
    return tuple(outs)
```

```python
import functools

import numpy as np
import jax
import jax.numpy as jnp
from jax import lax
from jax.experimental import pallas as pl
from jax.experimental.pallas import tpu as pltpu

D_MODEL = 2048
SEQ = 4096
DEPTH = 2
GRID_W = 64
D_FF = 5632
RET_HEADS = 8
RET_DK = D_MODEL // RET_HEADS
RET_DV = 2 * RET_DK
RET_VDIM = RET_HEADS * RET_DV
RET_IN = 2 * D_MODEL + 3 * RET_VDIM
ROPE_BASE = 10000.0
NA_HEADS = 16
NA_HEAD_DIM = D_MODEL // NA_HEADS
NA_KH = 8
NA_KW = 16
RMS_EPS = 1e-6
MASK_VALUE = -1e30

F32 = jnp.float32
BF16 = jnp.bfloat16

V7X_VMEM_BYTES = 64 * 1024 * 1024
VMEM_LIMIT_BYTES = V7X_VMEM_BYTES - 8 * 1024 * 1024

FFN_TM, FFN_TF = 1024, 512
PROJ_TM, PROJ_TN = 1024, 2048
OUT_TM = 512
SCAN_TC = 2048
SCAN_CHUNK = 256
GRID_ROWS = SEQ // GRID_W


def _params(semantics):
    return pltpu.CompilerParams(dimension_semantics=semantics, vmem_limit_bytes=VMEM_LIMIT_BYTES)


def _rmsnorm(xf, g):
    ms = jnp.mean(xf * xf, axis=-1, keepdims=True)
    return xf * lax.rsqrt(ms + RMS_EPS) * g


def _silu(g):
    return g / (1.0 + jnp.exp(-g))


def _ffn_kernel(x_ref, g_ref, wg_ref, wu_ref, wo_ref, *rest, final_norm):
    if final_norm:
        gfin_ref, o_ref, h_ref = rest
    else:
        o_ref, h_ref = rest
    j = pl.program_id(1)

    @pl.when(j == 0)
    def _():
        x = x_ref[...]
        h_ref[...] = _rmsnorm(x, g_ref[...]).astype(BF16)
        o_ref[...] = x

    h = h_ref[...]
    g = jnp.dot(h, wg_ref[...], preferred_element_type=F32)
    u = jnp.dot(h, wu_ref[...], preferred_element_type=F32)
    a = (_silu(g) * u * 0.5).astype(BF16)
    o_ref[...] += jnp.dot(a, wo_ref[...], preferred_element_type=F32)

    if final_norm:
        @pl.when(j == pl.num_programs(1) - 1)
        def _():
            o_ref[...] = _rmsnorm(o_ref[...], gfin_ref[...])


def _ffn(x, gain, w_in, w_out, layer, final_gain=None):
    t = x.shape[0]
    nj = D_FF // FFN_TF
    in_specs = [
        pl.BlockSpec((FFN_TM, D_MODEL), lambda i, j: (i, 0)),
        pl.BlockSpec((1, D_MODEL), lambda i, j: (0, 0)),
        pl.BlockSpec((None, D_MODEL, FFN_TF), lambda i, j: (layer, 0, j)),
        pl.BlockSpec((None, D_MODEL, FFN_TF), lambda i, j: (layer, 0, nj + j)),
        pl.BlockSpec((None, FFN_TF, D_MODEL), lambda i, j: (layer, j, 0)),
    ]
    args = [x, gain.reshape(1, D_MODEL), w_in, w_in, w_out]
    if final_gain is not None:
        in_specs.append(pl.BlockSpec((1, D_MODEL), lambda i, j: (0, 0)))
        args.append(final_gain.reshape(1, D_MODEL))
    return pl.pallas_call(
        functools.partial(_ffn_kernel, final_norm=final_gain is not None),
        grid=(t // FFN_TM, nj),
        in_specs=in_specs,
        out_specs=pl.BlockSpec((FFN_TM, D_MODEL), lambda i, j: (i, 0)),
        out_shape=jax.ShapeDtypeStruct((t, D_MODEL), F32),
        scratch_shapes=[pltpu.VMEM((FFN_TM, D_MODEL), BF16)],
        compiler_params=_params(("parallel", "arbitrary")),
        name="swiglu_ffn",
    )(*args)


def _proj_kernel(x_ref, g_ref, w_ref, *rest, rotary):
    if rotary:
        cos_ref, sin_ref, o_ref, h_ref = rest
    else:
        o_ref, h_ref = rest
    n = pl.program_id(1)

    @pl.when(n == 0)
    def _():
        h_ref[...] = _rmsnorm(x_ref[...], g_ref[...]).astype(BF16)

    y = jnp.dot(h_ref[...], w_ref[...], preferred_element_type=F32)
    if not rotary:
        o_ref[...] = y.astype(BF16)
        return
    cos = cos_ref[...]
    sin = sin_ref[...]
    scale = jnp.where(n < pl.num_programs(1) // 2, 1.0, RET_DK ** -0.5).astype(F32)
    half = RET_DK // 2
    for hh in range(PROJ_TN // RET_DK):
        x1 = y[:, hh * RET_DK: hh * RET_DK + half]
        x2 = y[:, hh * RET_DK + half: (hh + 1) * RET_DK]
        o_ref[:, hh * RET_DK: hh * RET_DK + half] = ((x1 * cos - x2 * sin) * scale).astype(BF16)
        o_ref[:, hh * RET_DK + half: (hh + 1) * RET_DK] = ((x1 * sin + x2 * cos) * scale).astype(BF16)


def _proj(x, gain, w, layer, cols, rotary=None):
    t = x.shape[0]
    n_out = cols[1] - cols[0]
    col0 = cols[0] // PROJ_TN
    in_specs = [
        pl.BlockSpec((PROJ_TM, D_MODEL), lambda i, n: (i, 0)),
        pl.BlockSpec((1, D_MODEL), lambda i, n: (0, 0)),
        pl.BlockSpec((None, D_MODEL, PROJ_TN), lambda i, n: (layer, 0, col0 + n)),
    ]
    args = [x, gain.reshape(1, D_MODEL), w]
    if rotary is not None:
        seq_tiles = SEQ // PROJ_TM
        tab_spec = pl.BlockSpec((PROJ_TM, RET_DK // 2), lambda i, n: (i % seq_tiles, 0))
        in_specs += [tab_spec, tab_spec]
        args += list(rotary)
    return pl.pallas_call(
        functools.partial(_proj_kernel, rotary=rotary is not None),
        grid=(t // PROJ_TM, n_out // PROJ_TN),
        in_specs=in_specs,
        out_specs=pl.BlockSpec((PROJ_TM, PROJ_TN), lambda i, n: (i, n)),
        out_shape=jax.ShapeDtypeStruct((t, n_out), BF16),
        scratch_shapes=[pltpu.VMEM((PROJ_TM, D_MODEL), BF16)],
        compiler_params=_params(("parallel", "arbitrary")),
        name="norm_proj_rotary" if rotary is not None else "norm_proj",
    )(*args)


def _out_kernel(a_ref, w_ref, x_ref, o_ref):
    o_ref[...] = x_ref[...] + jnp.dot(a_ref[...], w_ref[...], preferred_element_type=F32)


def _out_proj(a, w, layer, x):
    t, k = a.shape
    return pl.pallas_call(
        _out_kernel,
        grid=(t // OUT_TM,),
        in_specs=[
            pl.BlockSpec((OUT_TM, k), lambda i: (i, 0)),
            pl.BlockSpec((None, k, D_MODEL), lambda i: (layer, 0, 0), pipeline_mode=pl.Buffered(1)),
            pl.BlockSpec((OUT_TM, D_MODEL), lambda i: (i, 0)),
        ],
        out_specs=pl.BlockSpec((OUT_TM, D_MODEL), lambda i: (i, 0)),
        out_shape=jax.ShapeDtypeStruct((t, D_MODEL), F32),
        compiler_params=_params(("parallel",)),
        name="out_proj_residual",
    )(a, w, x)


SCAN_NBLK = SEQ // SCAN_TC
SCAN_CHUNKS = SCAN_TC // SCAN_CHUNK


def _scan_kernel(dec_ref, q_ref, k_ref, v_ref, gf_ref, gb_ref, tab_ref, gnf_ref, gnb_ref,
                 o_ref, r_ref, ob_ref):
    h = pl.program_id(1)
    t = pl.program_id(2)
    c_len = SCAN_CHUNK

    @pl.when((t == 0) | (t == SCAN_NBLK))
    def _():
        r_ref[...] = jnp.zeros_like(r_ref)

    def run(backward):
        inner = tab_ref[0]
        cross = jnp.concatenate([tab_ref[1]] * (RET_DV // c_len), axis=1)
        into = jnp.concatenate([tab_ref[2]] * (RET_DK // c_len), axis=1)
        dec = dec_ref[1 if backward else 0, h]
        gate_ref = gb_ref if backward else gf_ref
        gn = (gnb_ref if backward else gnf_ref)[...]
        blk = (SCAN_NBLK - 1 - t) if backward else (t - SCAN_NBLK)
        order = range(SCAN_CHUNKS - 1, -1, -1) if backward else range(SCAN_CHUNKS)
        for c in order:
            rows = pl.ds(c * c_len, c_len)
            q = q_ref[rows, :]
            k = k_ref[rows, :]
            v = v_ref[rows, :]
            s = lax.dot_general(q, k, (((1,), (1,)), ((), ())), preferred_element_type=F32) * inner
            kd = (k.astype(F32) * into).astype(BF16)
            r = r_ref[...]
            sv = jnp.dot(jnp.concatenate([s.astype(BF16), kd.T], axis=0), v, preferred_element_type=F32)
            o = sv[:c_len] + jnp.dot(q, r.astype(BF16), preferred_element_type=F32) * cross
            r_ref[...] = r * dec + sv[c_len:]
            on = o * lax.rsqrt(jnp.mean(o * o, axis=-1, keepdims=True) + RMS_EPS) * gn
            m = _silu(gate_ref[rows, :].astype(F32)) * on
            ob_rows = pl.ds(pl.multiple_of(blk * SCAN_TC + c * c_len, c_len), c_len)
            if backward:
                ob_ref[ob_rows, :] = m
            else:
                o_ref[rows, :] = (m + ob_ref[ob_rows, :]).astype(BF16)

    @pl.when(t < SCAN_NBLK)
    def _():
        o_ref[...] = jnp.zeros_like(o_ref)
        run(True)

    @pl.when(t >= SCAN_NBLK)
    def _():
        run(False)


def _retention_scan(qk, vg, tabs, dec, gn_f, gn_b):
    t = qk.shape[0]
    batch = t // SEQ
    nb = SCAN_NBLK

    def bwd_blk(s):
        return jnp.where(s < nb, nb - 1 - s, 0)

    def fwd_blk(s):
        return jnp.where(s < nb, 0, s - nb)

    def cur_blk(s):
        return jnp.where(s < nb, nb - 1 - s, s - nb)

    nh = RET_HEADS
    in_specs = [
        pl.BlockSpec(memory_space=pltpu.SMEM),
        pl.BlockSpec((SCAN_TC, RET_DK), lambda b, h, s: (b * nb + cur_blk(s), h)),
        pl.BlockSpec((SCAN_TC, RET_DK), lambda b, h, s: (b * nb + cur_blk(s), nh + h)),
        pl.BlockSpec((SCAN_TC, RET_DV), lambda b, h, s: (b * nb + cur_blk(s), h)),
        pl.BlockSpec((SCAN_TC, RET_DV), lambda b, h, s: (b * nb + fwd_blk(s), nh + h)),
        pl.BlockSpec((SCAN_TC, RET_DV), lambda b, h, s: (b * nb + bwd_blk(s), 2 * nh + h)),
        pl.BlockSpec((None, None, 3, SCAN_CHUNK, SCAN_CHUNK),
                     lambda b, h, s: (jnp.where(s < nb, 1, 0), h, 0, 0, 0)),
        pl.BlockSpec((1, RET_DV), lambda b, h, s: (0, h)),
        pl.BlockSpec((1, RET_DV), lambda b, h, s: (0, h)),
    ]
    return pl.pallas_call(
        _scan_kernel,
        grid=(batch, RET_HEADS, 2 * nb),
        in_specs=in_specs,
        out_specs=pl.BlockSpec((SCAN_TC, RET_DV), lambda b, h, s: (b * nb + fwd_blk(s), h)),
        out_shape=jax.ShapeDtypeStruct((t, RET_VDIM), BF16),
        scratch_shapes=[pltpu.VMEM((RET_DK, RET_DV), F32), pltpu.VMEM((SEQ, RET_DV), F32)],
        compiler_params=_params(("arbitrary", "arbitrary", "arbitrary")),
        name="retention_scan",
    )(dec, qk, qk, vg, vg, vg, tabs, gn_f.reshape(1, RET_VDIM), gn_b.reshape(1, RET_VDIM))


def _retention_tables(decay_f, decay_b):
    c = SCAN_CHUNK
    pos = jnp.arange(c, dtype=F32)
    ones = jnp.ones((1, 1, c), F32)

    def tables(log_gamma, backward):
        lg = log_gamma[:, None, None]
        diff = pos[:, None] - pos[None, :]
        if backward:
            diff = -diff
        inner = jnp.where(diff[None] >= 0, jnp.exp(lg * jnp.maximum(diff, 0.0)[None]), 0.0)
        cross_pos = (c - pos) if backward else (pos + 1.0)
        into_pos = pos if backward else (c - 1.0 - pos)
        cross = jnp.exp(log_gamma[:, None] * cross_pos[None])[:, :, None] * ones
        into = jnp.exp(log_gamma[:, None] * into_pos[None])[:, :, None] * ones
        return jnp.stack([inner, cross, into], axis=1), jnp.exp(log_gamma * c)

    lg_f = jnp.log1p(-jnp.exp(decay_f.astype(F32)))
    lg_b = jnp.log1p(-jnp.exp(decay_b.astype(F32)))
    tab_f, dec_f = tables(lg_f, False)
    tab_b, dec_b = tables(lg_b, True)
    return jnp.stack([tab_f, tab_b]), jnp.stack([dec_f, dec_b])


def _rotary_tables():
    d = RET_DK
    inv = ROPE_BASE ** (-jnp.arange(0, d, 2, dtype=F32) / d)
    ang = jnp.arange(SEQ, dtype=F32)[:, None] * inv[None, :]
    return jnp.cos(ang), jnp.sin(ang)


NA_QROWS = 4
NA_GROUPS = GRID_ROWS // NA_QROWS
NA_SLAB_ROWS = NA_QROWS + NA_KH
NA_QTOK = NA_QROWS * GRID_W
NA_SLAB = NA_SLAB_ROWS * GRID_W


def _na_kernel(q_ref, k_ref, v_ref, bias_ref, o_ref):
    scale = NA_HEAD_DIM ** -0.5

    def group(g, carry):
        s0 = jnp.clip(NA_QROWS * g - NA_KH // 2, 0, GRID_ROWS - NA_SLAB_ROWS)
        kind = jnp.where(g == 0, 0, jnp.where(g == NA_GROUPS - 1, 2, 1))
        rows = pl.ds(pl.multiple_of(g * NA_QTOK, NA_QTOK), NA_QTOK)
        slab = pl.ds(pl.multiple_of(s0 * GRID_W, NA_QTOK), NA_SLAB)
        s = lax.dot_general(q_ref[rows, :], k_ref[slab, :], (((1,), (1,)), ((), ())),
                            preferred_element_type=F32)
        s = s * scale + bias_ref[kind]
        e = jnp.exp(s - jnp.max(s, axis=-1, keepdims=True))
        o = jnp.dot(e.astype(BF16), v_ref[slab, :], preferred_element_type=F32)
        o_ref[rows, :] = (o * (1.0 / jnp.sum(e, axis=-1, keepdims=True))).astype(BF16)
        return carry

    lax.fori_loop(0, NA_GROUPS, group, 0, unroll=8)


def _na_bias_tables(rpb):
    pad = GRID_W - NA_KW
    padded = jnp.pad(rpb.astype(F32), ((0, 0), (0, 0), (pad, pad)))
    toe = jnp.stack([padded[:, :, GRID_W - 1 - c: 2 * GRID_W - 1 - c] for c in range(GRID_W)], axis=2)
    cq = np.arange(GRID_W)
    ck = np.arange(GRID_W)
    qwin = np.clip(cq - NA_KW // 2, 0, GRID_W - NA_KW)
    col_ok = (ck[None, :] >= qwin[:, None]) & (ck[None, :] < qwin[:, None] + NA_KW)
    toe = jnp.where(col_ok[None, None], toe, MASK_VALUE)
    masked = jnp.full((NA_HEADS, GRID_W, GRID_W), MASK_VALUE, F32)

    geometries = (
        lambda rq, rk: (rk < NA_KH, rk - rq + NA_KH - 1),
        lambda rq, rk: (rq <= rk < rq + NA_KH, rk - rq + NA_KH // 2 - 1),
        lambda rq, rk: (rk >= NA_QROWS, rk - rq - 1),
    )
    tables = []
    for geometry in geometries:
        rows = []
        for rq in range(NA_QROWS):
            blocks = []
            for rk in range(NA_SLAB_ROWS):
                ok, dr = geometry(rq, rk)
                blocks.append(toe[:, dr] if ok else masked)
            rows.append(jnp.concatenate(blocks, axis=2))
        tables.append(jnp.concatenate(rows, axis=1))
    return jnp.stack(tables, axis=1)


def _neighbourhood_attention(qkv, bias):
    t = qkv.shape[0]
    batch = t // SEQ
    hd = NA_HEAD_DIM
    return pl.pallas_call(
        _na_kernel,
        grid=(NA_HEADS, batch),
        in_specs=[
            pl.BlockSpec((SEQ, hd), lambda h, b: (b, h)),
            pl.BlockSpec((SEQ, hd), lambda h, b: (b, NA_HEADS + h)),
            pl.BlockSpec((SEQ, hd), lambda h, b: (b, 2 * NA_HEADS + h)),
            pl.BlockSpec((None, 3, NA_QTOK, NA_SLAB), lambda h, b: (h, 0, 0, 0)),
        ],
        out_specs=pl.BlockSpec((SEQ, hd), lambda h, b: (b, h)),
        out_shape=jax.ShapeDtypeStruct((t, D_MODEL), BF16),
        compiler_params=_params(("arbitrary", "arbitrary")),
        name="neighbourhood_attention",
    )(qkv, qkv, qkv, bias)


def _trunk(x, p):
    for i in range(DEPTH):
        x = _ffn(x, p["norm_ffn1"][i], p["ffn1_w_in"], p["ffn1_w_out"], i)
        j = i // 2
        if i % 2 == 0:
            qk = _proj(x, p["norm_mix"][i], p["ret_w_in"], j, (0, 2 * D_MODEL), rotary=p["rotary"])
            vg = _proj(x, p["norm_mix"][i], p["ret_w_in"], j, (2 * D_MODEL, RET_IN))
            tabs, dec = p["ret_tables"][j]
            mixed = _retention_scan(qk, vg, tabs, dec, p["ret_gn_f"][j], p["ret_gn_b"][j])
            x = _out_proj(mixed, p["ret_w_out"], j, x)
        else:
            qkv = _proj(x, p["norm_mix"][i], p["na_w_in"], j, (0, 3 * D_MODEL))
            att = _neighbourhood_attention(qkv, p["na_bias"][j])
            x = _out_proj(att, p["na_w_out"], j, x)
        x = _ffn(x, p["norm_ffn2"][i], p["ffn2_w_in"], p["ffn2_w_out"], i,
                 final_gain=p["norm_final"] if i == DEPTH - 1 else None)
    return x


def kernel(x_prompt, x_sample, norm_ffn1, ffn1_w_in, ffn1_w_out, norm_mix, norm_ffn2, ffn2_w_in, ffn2_w_out,
           ret_w_in, ret_w_out, ret_decay_f, ret_decay_b, ret_gn_f, ret_gn_b, na_w_in, na_w_out, na_rpb,
           norm_final):
    n_ret = ret_w_in.shape[0]
    n_na = na_w_in.shape[0]
    bf = lambda w: w.astype(BF16)
    p = dict(
        norm_ffn1=norm_ffn1, norm_mix=norm_mix, norm_ffn2=norm_ffn2, norm_final=norm_final,
        ffn1_w_in=bf(ffn1_w_in), ffn1_w_out=bf(ffn1_w_out), ffn2_w_in=bf(ffn2_w_in), ffn2_w_out=bf(ffn2_w_out),
        ret_w_in=bf(ret_w_in), ret_w_out=bf(ret_w_out), ret_gn_f=ret_gn_f, ret_gn_b=ret_gn_b,
        ret_tables=[_retention_tables(ret_decay_f[j], ret_decay_b[j]) for j in range(n_ret)],
        rotary=_rotary_tables(),
        na_w_in=bf(na_w_in), na_w_out=bf(na_w_out),
        na_bias=[_na_bias_tables(na_rpb[j]) for j in range(n_na)],
    )
    outs = []
    for x in (x_prompt, x_sample):
        y = _trunk(x.reshape(-1, D_MODEL), p)
        outs.append(y.reshape(x.shape))
    return tuple(outs)
```

```python
import functools

import numpy as np
import jax
import jax.numpy as jnp
from jax import lax
from jax.experimental import pallas as pl
from jax.experimental.pallas import tpu as pltpu

D_MODEL = 2048
SEQ = 4096
DEPTH = 2
GRID_W = 64
D_FF = 5632
RET_HEADS = 8
RET_DK = D_MODEL // RET_HEADS
RET_DV = 2 * RET_DK
RET_VDIM = RET_HEADS * RET_DV
RET_IN = 2 * D_MODEL + 3 * RET_VDIM
ROPE_BASE = 10000.0
NA_HEADS = 16
NA_HEAD_DIM = D_MODEL // NA_HEADS
NA_KH = 8
NA_KW = 16
RMS_EPS = 1e-6
MASK_VALUE = -1e30

F32 = jnp.float32
BF16 = jnp.bfloat16

V7X_VMEM_BYTES = 64 * 1024 * 1024
VMEM_LIMIT_BYTES = V7X_VMEM_BYTES - 8 * 1024 * 1024

FFN_TM, FFN_TF = 1024, 512
PROJ_TM, PROJ_TN = 1024, 2048
OUT_TM = 512
SCAN_TC = 2048
SCAN_CHUNK = 256
GRID_ROWS = SEQ // GRID_W


def _params(semantics):
    return pltpu.CompilerParams(dimension_semantics=semantics, vmem_limit_bytes=VMEM_LIMIT_BYTES)


def _rmsnorm(xf, g):
    ms = jnp.mean(xf * xf, axis=-1, keepdims=True)
    return xf * lax.rsqrt(ms + RMS_EPS) * g


def _silu(g):
    return g / (1.0 + jnp.exp(-g))


def _ffn_kernel(x_ref, g_ref, wg_ref, wu_ref, wo_ref, *rest, final_norm):
    if final_norm:
        gfin_ref, o_ref, h_ref = rest
    else:
        o_ref, h_ref = rest
    j = pl.program_id(1)

    @pl.when(j == 0)
    def _():
        x = x_ref[...]
        h_ref[...] = _rmsnorm(x, g_ref[...]).astype(BF16)
        o_ref[...] = x

    h = h_ref[...]
    g = jnp.dot(h, wg_ref[...], preferred_element_type=F32)
    u = jnp.dot(h, wu_ref[...], preferred_element_type=F32)
    a = (_silu(g) * u * 0.5).astype(BF16)
    o_ref[...] += jnp.dot(a, wo_ref[...], preferred_element_type=F32)

    if final_norm:
        @pl.when(j == pl.num_programs(1) - 1)
        def _():
            o_ref[...] = _rmsnorm(o_ref[...], gfin_ref[...])


def _ffn(x, gain, w_in, w_out, layer, final_gain=None):
    t = x.shape[0]
    nj = D_FF // FFN_TF
    in_specs = [
        pl.BlockSpec((FFN_TM, D_MODEL), lambda i, j: (i, 0)),
        pl.BlockSpec((1, D_MODEL), lambda i, j: (0, 0)),
        pl.BlockSpec((None, D_MODEL, FFN_TF), lambda i, j: (layer, 0, j)),
        pl.BlockSpec((None, D_MODEL, FFN_TF), lambda i, j: (layer, 0, nj + j)),
        pl.BlockSpec((None, FFN_TF, D_MODEL), lambda i, j: (layer, j, 0)),
    ]
    args = [x, gain.reshape(1, D_MODEL), w_in, w_in, w_out]
    if final_gain is not None:
        in_specs.append(pl.BlockSpec((1, D_MODEL), lambda i, j: (0, 0)))
        args.append(final_gain.reshape(1, D_MODEL))
    return pl.pallas_call(
        functools.partial(_ffn_kernel, final_norm=final_gain is not None),
        grid=(t // FFN_TM, nj),
        in_specs=in_specs,
        out_specs=pl.BlockSpec((FFN_TM, D_MODEL), lambda i, j: (i, 0)),
        out_shape=jax.ShapeDtypeStruct((t, D_MODEL), F32),
        scratch_shapes=[pltpu.VMEM((FFN_TM, D_MODEL), BF16)],
        compiler_params=_params(("parallel", "arbitrary")),
        name="swiglu_ffn",
    )(*args)


def _proj_kernel(x_ref, g_ref, w_ref, *rest, rotary):
    if rotary:
        cos_ref, sin_ref, o_ref, h_ref = rest
    else:
        o_ref, h_ref = rest
    n = pl.program_id(1)

    @pl.when(n == 0)
    def _():
        h_ref[...] = _rmsnorm(x_ref[...], g_ref[...]).astype(BF16)

    y = jnp.dot(h_ref[...], w_ref[...], preferred_element_type=F32)
    if not rotary:
        o_ref[...] = y.astype(BF16)
        return
    cos = cos_ref[...]
    sin = sin_ref[...]
    scale = jnp.where(n < pl.num_programs(1) // 2, 1.0, RET_DK ** -0.5).astype(F32)
    half = RET_DK // 2
    for hh in range(PROJ_TN // RET_DK):
        x1 = y[:, hh * RET_DK: hh * RET_DK + half]
        x2 = y[:, hh * RET_DK + half: (hh + 1) * RET_DK]
        o_ref[:, hh * RET_DK: hh * RET_DK + half] = ((x1 * cos - x2 * sin) * scale).astype(BF16)
        o_ref[:, hh * RET_DK + half: (hh + 1) * RET_DK] = ((x1 * sin + x2 * cos) * scale).astype(BF16)


def _proj(x, gain, w, layer, cols, rotary=None):
    t = x.shape[0]
    n_out = cols[1] - cols[0]
    col0 = cols[0] // PROJ_TN
    in_specs = [
        pl.BlockSpec((PROJ_TM, D_MODEL), lambda i, n: (i, 0)),
        pl.BlockSpec((1, D_MODEL), lambda i, n: (0, 0)),
        pl.BlockSpec((None, D_MODEL, PROJ_TN), lambda i, n: (layer, 0, col0 + n)),
    ]
    args = [x, gain.reshape(1, D_MODEL), w]
    if rotary is not None:
        seq_tiles = SEQ // PROJ_TM
        tab_spec = pl.BlockSpec((PROJ_TM, RET_DK // 2), lambda i, n: (i % seq_tiles, 0))
        in_specs += [tab_spec, tab_spec]
        args += list(rotary)
    return pl.pallas_call(
        functools.partial(_proj_kernel, rotary=rotary is not None),
        grid=(t // PROJ_TM, n_out // PROJ_TN),
        in_specs=in_specs,
        out_specs=pl.BlockSpec((PROJ_TM, PROJ_TN), lambda i, n: (i, n)),
        out_shape=jax.ShapeDtypeStruct((t, n_out), BF16),
        scratch_shapes=[pltpu.VMEM((PROJ_TM, D_MODEL), BF16)],
        compiler_params=_params(("parallel", "arbitrary")),
        name="norm_proj_rotary" if rotary is not None else "norm_proj",
    )(*args)


def _out_kernel(a_ref, w_ref, x_ref, o_ref):
    o_ref[...] = x_ref[...] + jnp.dot(a_ref[...], w_ref[...], preferred_element_type=F32)


def _out_proj(a, w, layer, x):
    t, k = a.shape
    return pl.pallas_call(
        _out_kernel,
        grid=(t // OUT_TM,),
        in_specs=[
            pl.BlockSpec((OUT_TM, k), lambda i: (i, 0)),
            pl.BlockSpec((None, k, D_MODEL), lambda i: (layer, 0, 0), pipeline_mode=pl.Buffered(1)),
            pl.BlockSpec((OUT_TM, D_MODEL), lambda i: (i, 0)),
        ],
        out_specs=pl.BlockSpec((OUT_TM, D_MODEL), lambda i: (i, 0)),
        out_shape=jax.ShapeDtypeStruct((t, D_MODEL), F32),
        compiler_params=_params(("parallel",)),
        name="out_proj_residual",
    )(a, w, x)


SCAN_NBLK = SEQ // SCAN_TC
SCAN_CHUNKS = SCAN_TC // SCAN_CHUNK


def _scan_kernel(dec_ref, q_ref, k_ref, v_ref, gf_ref, gb_ref, tab_ref, gnf_ref, gnb_ref,
                 o_ref, r_ref, ob_ref):
    h = pl.program_id(1)
    t = pl.program_id(2)
    c_len = SCAN_CHUNK

    @pl.when((t == 0) | (t == SCAN_NBLK))
    def _():
        r_ref[...] = jnp.zeros_like(r_ref)

    def run(backward):
        inner = tab_ref[0]
        cross = jnp.concatenate([tab_ref[1]] * (RET_DV // c_len), axis=1)
        into = jnp.concatenate([tab_ref[2]] * (RET_DK // c_len), axis=1)
        dec = dec_ref[1 if backward else 0, h]
        gate_ref = gb_ref if backward else gf_ref
        gn = (gnb_ref if backward else gnf_ref)[...]
        blk = (SCAN_NBLK - 1 - t) if backward else (t - SCAN_NBLK)
        order = range(SCAN_CHUNKS - 1, -1, -1) if backward else range(SCAN_CHUNKS)
        for c in order:
            rows = pl.ds(c * c_len, c_len)
            q = q_ref[rows, :]
            k = k_ref[rows, :]
            v = v_ref[rows, :]
            s = lax.dot_general(q, k, (((1,), (1,)), ((), ())), preferred_element_type=F32) * inner
            kd = (k.astype(F32) * into).astype(BF16)
            r = r_ref[...]
            sv = jnp.dot(jnp.concatenate([s.astype(BF16), kd.T], axis=0), v, preferred_element_type=F32)
            o = sv[:c_len] + jnp.dot(q, r.astype(BF16), preferred_element_type=F32) * cross
            r_ref[...] = r * dec + sv[c_len:]
            on = o * lax.rsqrt(jnp.mean(o * o, axis=-1, keepdims=True) + RMS_EPS) * gn
            m = _silu(gate_ref[rows, :].astype(F32)) * on
            ob_rows = pl.ds(pl.multiple_of(blk * SCAN_TC + c * c_len, c_len), c_len)
            if backward:
                ob_ref[ob_rows, :] = m
            else:
                o_ref[rows, :] = (m + ob_ref[ob_rows, :]).astype(BF16)

    @pl.when(t < SCAN_NBLK)
    def _():
        o_ref[...] = jnp.zeros_like(o_ref)
        run(True)

    @pl.when(t >= SCAN_NBLK)
    def _():
        run(False)


def _retention_scan(qk, vg, tabs, dec, gn_f, gn_b):
    t = qk.shape[0]
    batch = t // SEQ
    nb = SCAN_NBLK

    def bwd_blk(s):
        return jnp.where(s < nb, nb - 1 - s, 0)

    def fwd_blk(s):
        return jnp.where(s < nb, 0, s - nb)

    def cur_blk(s):
        return jnp.where(s < nb, nb - 1 - s, s - nb)

    nh = RET_HEADS
    in_specs = [
        pl.BlockSpec(memory_space=pltpu.SMEM),
        pl.BlockSpec((SCAN_TC, RET_DK), lambda b, h, s: (b * nb + cur_blk(s), h)),
        pl.BlockSpec((SCAN_TC, RET_DK), lambda b, h, s: (b * nb + cur_blk(s), nh + h)),
        pl.BlockSpec((SCAN_TC, RET_DV), lambda b, h, s: (b * nb + cur_blk(s), h)),
        pl.BlockSpec((SCAN_TC, RET_DV), lambda b, h, s: (b * nb + fwd_blk(s), nh + h)),
        pl.BlockSpec((SCAN_TC, RET_DV), lambda b, h, s: (b * nb + bwd_blk(s), 2 * nh + h)),
        pl.BlockSpec((None, None, 3, SCAN_CHUNK, SCAN_CHUNK),
                     lambda b, h, s: (jnp.where(s < nb, 1, 0), h, 0, 0, 0)),
        pl.BlockSpec((1, RET_DV), lambda b, h, s: (0, h)),
        pl.BlockSpec((1, RET_DV), lambda b, h, s: (0, h)),
    ]
    return pl.pallas_call(
        _scan_kernel,
        grid=(batch, RET_HEADS, 2 * nb),
        in_specs=in_specs,
        out_specs=pl.BlockSpec((SCAN_TC, RET_DV), lambda b, h, s: (b * nb + fwd_blk(s), h)),
        out_shape=jax.ShapeDtypeStruct((t, RET_VDIM), BF16),
        scratch_shapes=[pltpu.VMEM((RET_DK, RET_DV), F32), pltpu.VMEM((SEQ, RET_DV), F32)],
        compiler_params=_params(("arbitrary", "arbitrary", "arbitrary")),
        name="retention_scan",
    )(dec, qk, qk, vg, vg, vg, tabs, gn_f.reshape(1, RET_VDIM), gn_b.reshape(1, RET_VDIM))


def _retention_tables(decay_f, decay_b):
    c = SCAN_CHUNK
    pos = jnp.arange(c, dtype=F32)
    ones = jnp.ones((1, 1, c), F32)

    def tables(log_gamma, backward):
        lg = log_gamma[:, None, None]
        diff = pos[:, None] - pos[None, :]
        if backward:
            diff = -diff
        inner = jnp.where(diff[None] >= 0, jnp.exp(lg * jnp.maximum(diff, 0.0)[None]), 0.0)
        cross_pos = (c - pos) if backward else (pos + 1.0)
        into_pos = pos if backward else (c - 1.0 - pos)
        cross = jnp.exp(log_gamma[:, None] * cross_pos[None])[:, :, None] * ones
        into = jnp.exp(log_gamma[:, None] * into_pos[None])[:, :, None] * ones
        return jnp.stack([inner, cross, into], axis=1), jnp.exp(log_gamma * c)

    lg_f = jnp.log1p(-jnp.exp(decay_f.astype(F32)))
    lg_b = jnp.log1p(-jnp.exp(decay_b.astype(F32)))
    tab_f, dec_f = tables(lg_f, False)
    tab_b, dec_b = tables(lg_b, True)
    return jnp.stack([tab_f, tab_b]), jnp.stack([dec_f, dec_b])


def _rotary_tables():
    d = RET_DK
    inv = ROPE_BASE ** (-jnp.arange(0, d, 2, dtype=F32) / d)
    ang = jnp.arange(SEQ, dtype=F32)[:, None] * inv[None, :]
    return jnp.cos(ang), jnp.sin(ang)


NA_QROWS = 4
NA_GROUPS = GRID_ROWS // NA_QROWS
NA_SLAB_ROWS = NA_QROWS + NA_KH
NA_QTOK = NA_QROWS * GRID_W
NA_SLAB = NA_SLAB_ROWS * GRID_W


def _na_kernel(q_ref, k_ref, v_ref, bias_ref, o_ref, s_ref):
    scale = NA_HEAD_DIM ** -0.5

    def windows(g):
        s0 = jnp.clip(NA_QROWS * g - NA_KH // 2, 0, GRID_ROWS - NA_SLAB_ROWS)
        rows = pl.ds(pl.multiple_of(g * NA_QTOK, NA_QTOK), NA_QTOK)
        slab = pl.ds(pl.multiple_of(s0 * GRID_W, NA_QTOK), NA_SLAB)
        return rows, slab

    def scores(g, slot):
        rows, slab = windows(g)
        kind = jnp.where(g == 0, 0, jnp.where(g == NA_GROUPS - 1, 2, 1))
        s = lax.dot_general(q_ref[rows, :], k_ref[slab, :], (((1,), (1,)), ((), ())),
                            preferred_element_type=F32)
        s_ref[slot] = s * scale + bias_ref[kind]

    def attend(g, slot):
        rows, slab = windows(g)
        s = s_ref[slot]
        e = jnp.exp(s - jnp.max(s, axis=-1, keepdims=True))
        o = jnp.dot(e.astype(BF16), v_ref[slab, :], preferred_element_type=F32)
        o_ref[rows, :] = (o * (1.0 / jnp.sum(e, axis=-1, keepdims=True))).astype(BF16)

    scores(0, 0)

    def pair(i, carry):
        g = 2 * i
        scores(g + 1, 1)
        attend(g, 0)
        scores(g + 2, 0)
        attend(g + 1, 1)
        return carry

    lax.fori_loop(0, NA_GROUPS // 2 - 1, pair, 0, unroll=True)
    scores(NA_GROUPS - 1, 1)
    attend(NA_GROUPS - 2, 0)
    attend(NA_GROUPS - 1, 1)


def _na_bias_tables(rpb):
    pad = GRID_W - NA_KW
    padded = jnp.pad(rpb.astype(F32), ((0, 0), (0, 0), (pad, pad)))
    toe = jnp.stack([padded[:, :, GRID_W - 1 - c: 2 * GRID_W - 1 - c] for c in range(GRID_W)], axis=2)
    cq = np.arange(GRID_W)
    ck = np.arange(GRID_W)
    qwin = np.clip(cq - NA_KW // 2, 0, GRID_W - NA_KW)
    col_ok = (ck[None, :] >= qwin[:, None]) & (ck[None, :] < qwin[:, None] + NA_KW)
    toe = jnp.where(col_ok[None, None], toe, MASK_VALUE)
    masked = jnp.full((NA_HEADS, GRID_W, GRID_W), MASK_VALUE, F32)

    geometries = (
        lambda rq, rk: (rk < NA_KH, rk - rq + NA_KH - 1),
        lambda rq, rk: (rq <= rk < rq + NA_KH, rk - rq + NA_KH // 2 - 1),
        lambda rq, rk: (rk >= NA_QROWS, rk - rq - 1),
    )
    tables = []
    for geometry in geometries:
        rows = []
        for rq in range(NA_QROWS):
            blocks = []
            for rk in range(NA_SLAB_ROWS):
                ok, dr = geometry(rq, rk)
                blocks.append(toe[:, dr] if ok else masked)
            rows.append(jnp.concatenate(blocks, axis=2))
        tables.append(jnp.concatenate(rows, axis=1))
    return jnp.stack(tables, axis=1)


def _neighbourhood_attention(qkv, bias):
    t = qkv.shape[0]
    batch = t // SEQ
    hd = NA_HEAD_DIM
    return pl.pallas_call(
        _na_kernel,
        grid=(NA_HEADS, batch),
        in_specs=[
            pl.BlockSpec((SEQ, hd), lambda h, b: (b, h)),
            pl.BlockSpec((SEQ, hd), lambda h, b: (b, NA_HEADS + h)),
            pl.BlockSpec((SEQ, hd), lambda h, b: (b, 2 * NA_HEADS + h)),
            pl.BlockSpec((None, 3, NA_QTOK, NA_SLAB), lambda h, b: (h, 0, 0, 0)),
        ],
        out_specs=pl.BlockSpec((SEQ, hd), lambda h, b: (b, h)),
        out_shape=jax.ShapeDtypeStruct((t, D_MODEL), BF16),
        scratch_shapes=[pltpu.VMEM((2, NA_QTOK, NA_SLAB), F32)],
        compiler_params=_params(("arbitrary", "arbitrary")),
        name="neighbourhood_attention",
    )(qkv, qkv, qkv, bias)


def _trunk(x, p):
    for i in range(DEPTH):
        x = _ffn(x, p["norm_ffn1"][i], p["ffn1_w_in"], p["ffn1_w_out"], i)
        j = i // 2
        if i % 2 == 0:
            qk = _proj(x, p["norm_mix"][i], p["ret_w_in"], j, (0, 2 * D_MODEL), rotary=p["rotary"])
            vg = _proj(x, p["norm_mix"][i], p["ret_w_in"], j, (2 * D_MODEL, RET_IN))
            tabs, dec = p["ret_tables"][j]
            mixed = _retention_scan(qk, vg, tabs, dec, p["ret_gn_f"][j], p["ret_gn_b"][j])
            x = _out_proj(mixed, p["ret_w_out"], j, x)
        else:
            qkv = _proj(x, p["norm_mix"][i], p["na_w_in"], j, (0, 3 * D_MODEL))
            att = _neighbourhood_attention(qkv, p["na_bias"][j])
            x = _out_proj(att, p["na_w_out"], j, x)
        x = _ffn(x, p["norm_ffn2"][i], p["ffn2_w_in"], p["ffn2_w_out"], i,
                 final_gain=p["norm_final"] if i == DEPTH - 1 else None)
    return x


def kernel(x_prompt, x_sample, norm_ffn1, ffn1_w_in, ffn1_w_out, norm_mix, norm_ffn2, ffn2_w_in, ffn2_w_out,
           ret_w_in, ret_w_out, ret_decay_f, ret_decay_b, ret_gn_f, ret_gn_b, na_w_in, na_w_out, na_rpb,
           norm_final):
    n_ret = ret_w_in.shape[0]
    n_na = na_w_in.shape[0]
    bf = lambda w: w.astype(BF16)
    p = dict(
        norm_ffn1=norm_ffn1, norm_mix=norm_mix, norm_ffn2=norm_ffn2, norm_final=norm_final,
        ffn1_w_in=bf(ffn1_w_in), ffn1_w_out=bf(ffn1_w_out), ffn2_w_in=bf(ffn2_w_in), ffn2_w_out=bf(ffn2_w_out),
        ret_w_in=bf(ret_w_in), ret_w_out=bf(ret_w_out), ret_gn_f=ret_gn_f, ret_gn_b=ret_gn_b,
        ret_tables=[_retention_tables(ret_decay_f[j], ret_decay_b[j]) for j in range(n_ret)],
        rotary=_rotary_tables(),
        na_w_in=bf(na_w_in), na_w_out=bf(na_w_out),
        na_bias=[_na_bias_tables(na_rpb[j]) for j in range(n_na)],
    )
    outs = []
    for x in (x_prompt, x_sample):
        y = _trunk(x.reshape(-1, D_MODEL), p)
        outs.append(y.reshape(x.shape))
    return tuple(outs)
```

```python
import functools

import numpy as np
import jax
import jax.numpy as jnp
from jax import lax
from jax.experimental import pallas as pl
from jax.experimental.pallas import tpu as pltpu

D_MODEL = 2048
SEQ = 4096
DEPTH = 2
GRID_W = 64
D_FF = 5632
RET_HEADS = 8
RET_DK = D_MODEL // RET_HEADS
RET_DV = 2 * RET_DK
RET_VDIM = RET_HEADS * RET_DV
RET_IN = 2 * D_MODEL + 3 * RET_VDIM
ROPE_BASE = 10000.0
NA_HEADS = 16
NA_HEAD_DIM = D_MODEL // NA_HEADS
NA_KH = 8
NA_KW = 16
RMS_EPS = 1e-6
MASK_VALUE = -1e30
LOG2_E = 1.4426950408889634

F32 = jnp.float32
BF16 = jnp.bfloat16

V7X_VMEM_BYTES = 64 * 1024 * 1024
VMEM_LIMIT_BYTES = V7X_VMEM_BYTES - 8 * 1024 * 1024

FFN_TM, FFN_TF = 1024, 512
PROJ_TM, PROJ_TN = 1024, 2048
OUT_TM = 512
SCAN_TC = 2048
SCAN_CHUNK = 256
GRID_ROWS = SEQ // GRID_W


def _params(semantics):
    return pltpu.CompilerParams(dimension_semantics=semantics, vmem_limit_bytes=VMEM_LIMIT_BYTES)


def _rmsnorm(xf, g):
    ms = jnp.mean(xf * xf, axis=-1, keepdims=True)
    return xf * lax.rsqrt(ms + RMS_EPS) * g


def _silu(g):
    return g / (1.0 + jnp.exp(-g))


def _ffn_kernel(x_ref, g_ref, wg_ref, wu_ref, wo_ref, *rest, final_norm):
    if final_norm:
        gfin_ref, o_ref, h_ref = rest
    else:
        o_ref, h_ref = rest
    j = pl.program_id(1)

    @pl.when(j == 0)
    def _():
        x = x_ref[...]
        h_ref[...] = _rmsnorm(x, g_ref[...]).astype(BF16)
        o_ref[...] = x

    h = h_ref[...]
    g = jnp.dot(h, wg_ref[...], preferred_element_type=F32)
    u = jnp.dot(h, wu_ref[...], preferred_element_type=F32)
    a = (_silu(g) * u * 0.5).astype(BF16)
    o_ref[...] += jnp.dot(a, wo_ref[...], preferred_element_type=F32)

    if final_norm:
        @pl.when(j == pl.num_programs(1) - 1)
        def _():
            o_ref[...] = _rmsnorm(o_ref[...], gfin_ref[...])


def _ffn(x, gain, w_in, w_out, layer, final_gain=None):
    t = x.shape[0]
    nj = D_FF // FFN_TF
    in_specs = [
        pl.BlockSpec((FFN_TM, D_MODEL), lambda i, j: (i, 0)),
        pl.BlockSpec((1, D_MODEL), lambda i, j: (0, 0)),
        pl.BlockSpec((None, D_MODEL, FFN_TF), lambda i, j: (layer, 0, j)),
        pl.BlockSpec((None, D_MODEL, FFN_TF), lambda i, j: (layer, 0, nj + j)),
        pl.BlockSpec((None, FFN_TF, D_MODEL), lambda i, j: (layer, j, 0)),
    ]
    args = [x, gain.reshape(1, D_MODEL), w_in, w_in, w_out]
    if final_gain is not None:
        in_specs.append(pl.BlockSpec((1, D_MODEL), lambda i, j: (0, 0)))
        args.append(final_gain.reshape(1, D_MODEL))
    return pl.pallas_call(
        functools.partial(_ffn_kernel, final_norm=final_gain is not None),
        grid=(t // FFN_TM, nj),
        in_specs=in_specs,
        out_specs=pl.BlockSpec((FFN_TM, D_MODEL), lambda i, j: (i, 0)),
        out_shape=jax.ShapeDtypeStruct((t, D_MODEL), F32),
        scratch_shapes=[pltpu.VMEM((FFN_TM, D_MODEL), BF16)],
        compiler_params=_params(("parallel", "arbitrary")),
        name="swiglu_ffn",
    )(*args)


def _proj_kernel(x_ref, g_ref, w_ref, *rest, epilogue):
    if epilogue == "rotary":
        cos_ref, sin_ref, o_ref, h_ref = rest
    elif epilogue == "gate":
        gn_ref, o_ref, h_ref = rest
    else:
        o_ref, h_ref = rest
    n = pl.program_id(1)

    @pl.when(n == 0)
    def _():
        h_ref[...] = _rmsnorm(x_ref[...], g_ref[...]).astype(BF16)

    y = jnp.dot(h_ref[...], w_ref[...], preferred_element_type=F32)
    if epilogue == "plain":
        o_ref[...] = y.astype(BF16)
        return
    if epilogue == "gate":
        o_ref[...] = (_silu(y) * gn_ref[...]).astype(BF16)
        return
    cos = cos_ref[...]
    sin = sin_ref[...]
    scale = jnp.where(n < pl.num_programs(1) // 2, 1.0, RET_DK ** -0.5).astype(F32)
    half = RET_DK // 2
    for hh in range(PROJ_TN // RET_DK):
        x1 = y[:, hh * RET_DK: hh * RET_DK + half]
        x2 = y[:, hh * RET_DK + half: (hh + 1) * RET_DK]
        o_ref[:, hh * RET_DK: hh * RET_DK + half] = ((x1 * cos - x2 * sin) * scale).astype(BF16)
        o_ref[:, hh * RET_DK + half: (hh + 1) * RET_DK] = ((x1 * sin + x2 * cos) * scale).astype(BF16)


def _proj(x, gain, w, layer, cols, rotary=None, gate_gain=None):
    t = x.shape[0]
    n_out = cols[1] - cols[0]
    col0 = cols[0] // PROJ_TN
    in_specs = [
        pl.BlockSpec((PROJ_TM, D_MODEL), lambda i, n: (i, 0)),
        pl.BlockSpec((1, D_MODEL), lambda i, n: (0, 0)),
        pl.BlockSpec((None, D_MODEL, PROJ_TN), lambda i, n: (layer, 0, col0 + n)),
    ]
    args = [x, gain.reshape(1, D_MODEL), w]
    epilogue = "plain"
    if rotary is not None:
        epilogue = "rotary"
        seq_tiles = SEQ // PROJ_TM
        tab_spec = pl.BlockSpec((PROJ_TM, RET_DK // 2), lambda i, n: (i % seq_tiles, 0))
        in_specs += [tab_spec, tab_spec]
        args += list(rotary)
    elif gate_gain is not None:
        epilogue = "gate"
        in_specs.append(pl.BlockSpec((1, PROJ_TN), lambda i, n: (0, n)))
        args.append(gate_gain.reshape(1, n_out).astype(F32))
    return pl.pallas_call(
        functools.partial(_proj_kernel, epilogue=epilogue),
        grid=(t // PROJ_TM, n_out // PROJ_TN),
        in_specs=in_specs,
        out_specs=pl.BlockSpec((PROJ_TM, PROJ_TN), lambda i, n: (i, n)),
        out_shape=jax.ShapeDtypeStruct((t, n_out), BF16),
        scratch_shapes=[pltpu.VMEM((PROJ_TM, D_MODEL), BF16)],
        compiler_params=_params(("parallel", "arbitrary")),
        name="norm_proj_" + epilogue,
    )(*args)


def _out_kernel(a_ref, w_ref, x_ref, o_ref):
    o_ref[...] = x_ref[...] + jnp.dot(a_ref[...], w_ref[...], preferred_element_type=F32)


def _out_proj(a, w, layer, x):
    t, k = a.shape
    return pl.pallas_call(
        _out_kernel,
        grid=(t // OUT_TM,),
        in_specs=[
            pl.BlockSpec((OUT_TM, k), lambda i: (i, 0)),
            pl.BlockSpec((None, k, D_MODEL), lambda i: (layer, 0, 0), pipeline_mode=pl.Buffered(1)),
            pl.BlockSpec((OUT_TM, D_MODEL), lambda i: (i, 0)),
        ],
        out_specs=pl.BlockSpec((OUT_TM, D_MODEL), lambda i: (i, 0)),
        out_shape=jax.ShapeDtypeStruct((t, D_MODEL), F32),
        compiler_params=_params(("parallel",)),
        name="out_proj_residual",
    )(a, w, x)


SCAN_NBLK = SEQ // SCAN_TC
SCAN_CHUNKS = SCAN_TC // SCAN_CHUNK


def _scan_kernel(dec_ref, q_ref, k_ref, v_ref, gf_ref, gb_ref, tab_ref, o_ref, r_ref, ob_ref):
    h = pl.program_id(1)
    t = pl.program_id(2)
    c_len = SCAN_CHUNK

    @pl.when((t == 0) | (t == SCAN_NBLK))
    def _():
        r_ref[...] = jnp.zeros_like(r_ref)

    def run(backward):
        inner = tab_ref[0]
        cross = jnp.concatenate([tab_ref[1]] * (RET_DV // c_len), axis=1)
        into = jnp.concatenate([tab_ref[2]] * (RET_DK // c_len), axis=1)
        dec = dec_ref[1 if backward else 0, h]
        gate_ref = gb_ref if backward else gf_ref
        blk = (SCAN_NBLK - 1 - t) if backward else (t - SCAN_NBLK)
        order = range(SCAN_CHUNKS - 1, -1, -1) if backward else range(SCAN_CHUNKS)
        for c in order:
            rows = pl.ds(c * c_len, c_len)
            q = q_ref[rows, :]
            k = k_ref[rows, :]
            v = v_ref[rows, :]
            s = lax.dot_general(q, k, (((1,), (1,)), ((), ())), preferred_element_type=F32) * inner
            kd = (k.astype(F32) * into).astype(BF16)
            r = r_ref[...]
            sv = jnp.dot(jnp.concatenate([s.astype(BF16), kd.T], axis=0), v, preferred_element_type=F32)
            o = sv[:c_len] + jnp.dot(q, r.astype(BF16), preferred_element_type=F32) * cross
            r_ref[...] = r * dec + sv[c_len:]
            on = o * lax.rsqrt(jnp.mean(o * o, axis=-1, keepdims=True) + RMS_EPS)
            m = gate_ref[rows, :].astype(F32) * on
            ob_rows = pl.ds(pl.multiple_of(blk * SCAN_TC + c * c_len, c_len), c_len)
            if backward:
                ob_ref[ob_rows, :] = m
            else:
                o_ref[rows, :] = (m + ob_ref[ob_rows, :]).astype(BF16)

    @pl.when(t < SCAN_NBLK)
    def _():
        o_ref[...] = jnp.zeros_like(o_ref)
        run(True)

    @pl.when(t >= SCAN_NBLK)
    def _():
        run(False)


def _retention_scan(qk, v, gates, tabs, dec):
    t = qk.shape[0]
    batch = t // SEQ
    nb = SCAN_NBLK

    def bwd_blk(s):
        return jnp.where(s < nb, nb - 1 - s, 0)

    def fwd_blk(s):
        return jnp.where(s < nb, 0, s - nb)

    def cur_blk(s):
        return jnp.where(s < nb, nb - 1 - s, s - nb)

    nh = RET_HEADS
    in_specs = [
        pl.BlockSpec(memory_space=pltpu.SMEM),
        pl.BlockSpec((SCAN_TC, RET_DK), lambda b, h, s: (b * nb + cur_blk(s), h)),
        pl.BlockSpec((SCAN_TC, RET_DK), lambda b, h, s: (b * nb + cur_blk(s), nh + h)),
        pl.BlockSpec((SCAN_TC, RET_DV), lambda b, h, s: (b * nb + cur_blk(s), h)),
        pl.BlockSpec((SCAN_TC, RET_DV), lambda b, h, s: (b * nb + fwd_blk(s), h)),
        pl.BlockSpec((SCAN_TC, RET_DV), lambda b, h, s: (b * nb + bwd_blk(s), nh + h)),
        pl.BlockSpec((None, None, 3, SCAN_CHUNK, SCAN_CHUNK),
                     lambda b, h, s: (jnp.where(s < nb, 1, 0), h, 0, 0, 0)),
    ]
    return pl.pallas_call(
        _scan_kernel,
        grid=(batch, RET_HEADS, 2 * nb),
        in_specs=in_specs,
        out_specs=pl.BlockSpec((SCAN_TC, RET_DV), lambda b, h, s: (b * nb + fwd_blk(s), h)),
        out_shape=jax.ShapeDtypeStruct((t, RET_VDIM), BF16),
        scratch_shapes=[pltpu.VMEM((RET_DK, RET_DV), F32), pltpu.VMEM((SEQ, RET_DV), F32)],
        compiler_params=_params(("arbitrary", "arbitrary", "arbitrary")),
        name="retention_scan",
    )(dec, qk, qk, v, gates, gates, tabs)


def _retention_tables(decay_f, decay_b):
    c = SCAN_CHUNK
    pos = jnp.arange(c, dtype=F32)
    ones = jnp.ones((1, 1, c), F32)

    def tables(log_gamma, backward):
        lg = log_gamma[:, None, None]
        diff = pos[:, None] - pos[None, :]
        if backward:
            diff = -diff
        inner = jnp.where(diff[None] >= 0, jnp.exp(lg * jnp.maximum(diff, 0.0)[None]), 0.0)
        cross_pos = (c - pos) if backward else (pos + 1.0)
        into_pos = pos if backward else (c - 1.0 - pos)
        cross = jnp.exp(log_gamma[:, None] * cross_pos[None])[:, :, None] * ones
        into = jnp.exp(log_gamma[:, None] * into_pos[None])[:, :, None] * ones
        return jnp.stack([inner, cross, into], axis=1), jnp.exp(log_gamma * c)

    lg_f = jnp.log1p(-jnp.exp(decay_f.astype(F32)))
    lg_b = jnp.log1p(-jnp.exp(decay_b.astype(F32)))
    tab_f, dec_f = tables(lg_f, False)
    tab_b, dec_b = tables(lg_b, True)
    return jnp.stack([tab_f, tab_b]), jnp.stack([dec_f, dec_b])


def _rotary_tables():
    d = RET_DK
    inv = ROPE_BASE ** (-jnp.arange(0, d, 2, dtype=F32) / d)
    ang = jnp.arange(SEQ, dtype=F32)[:, None] * inv[None, :]
    return jnp.cos(ang), jnp.sin(ang)


NA_QROWS = 4
NA_GROUPS = GRID_ROWS // NA_QROWS
NA_SLAB_ROWS = NA_QROWS + NA_KH
NA_QTOK = NA_QROWS * GRID_W
NA_SLAB = NA_SLAB_ROWS * GRID_W


def _na_kernel(q_ref, k_ref, v_ref, bias_ref, o_ref, s_ref):
    scale = NA_HEAD_DIM ** -0.5 * LOG2_E

    def windows(g):
        s0 = jnp.clip(NA_QROWS * g - NA_KH // 2, 0, GRID_ROWS - NA_SLAB_ROWS)
        rows = pl.ds(pl.multiple_of(g * NA_QTOK, NA_QTOK), NA_QTOK)
        slab = pl.ds(pl.multiple_of(s0 * GRID_W, NA_QTOK), NA_SLAB)
        return rows, slab

    def scores(g, slot):
        rows, slab = windows(g)
        kind = jnp.where(g == 0, 0, jnp.where(g == NA_GROUPS - 1, 2, 1))
        s = lax.dot_general(q_ref[rows, :], k_ref[slab, :], (((1,), (1,)), ((), ())),
                            preferred_element_type=F32)
        s_ref[slot] = s * scale + bias_ref[kind]

    def attend(g, slot):
        rows, slab = windows(g)
        s = s_ref[slot]
        e = jnp.exp2(s - jnp.max(s, axis=-1, keepdims=True))
        o = jnp.dot(e.astype(BF16), v_ref[slab, :], preferred_element_type=F32)
        o_ref[rows, :] = (o * (1.0 / jnp.sum(e, axis=-1, keepdims=True))).astype(BF16)

    scores(0, 0)

    def pair(i, carry):
        g = 2 * i
        scores(g + 1, 1)
        attend(g, 0)
        scores(g + 2, 0)
        attend(g + 1, 1)
        return carry

    lax.fori_loop(0, NA_GROUPS // 2 - 1, pair, 0, unroll=True)
    scores(NA_GROUPS - 1, 1)
    attend(NA_GROUPS - 2, 0)
    attend(NA_GROUPS - 1, 1)


def _na_bias_tables(rpb):
    pad = GRID_W - NA_KW
    padded = jnp.pad(rpb.astype(F32), ((0, 0), (0, 0), (pad, pad)))
    toe = jnp.stack([padded[:, :, GRID_W - 1 - c: 2 * GRID_W - 1 - c] for c in range(GRID_W)], axis=2)
    cq = np.arange(GRID_W)
    ck = np.arange(GRID_W)
    qwin = np.clip(cq - NA_KW // 2, 0, GRID_W - NA_KW)
    col_ok = (ck[None, :] >= qwin[:, None]) & (ck[None, :] < qwin[:, None] + NA_KW)
    toe = jnp.where(col_ok[None, None], toe * LOG2_E, MASK_VALUE)
    masked = jnp.full((NA_HEADS, GRID_W, GRID_W), MASK_VALUE, F32)

    geometries = (
        lambda rq, rk: (rk < NA_KH, rk - rq + NA_KH - 1),
        lambda rq, rk: (rq <= rk < rq + NA_KH, rk - rq + NA_KH // 2 - 1),
        lambda rq, rk: (rk >= NA_QROWS, rk - rq - 1),
    )
    tables = []
    for geometry in geometries:
        rows = []
        for rq in range(NA_QROWS):
            blocks = []
            for rk in range(NA_SLAB_ROWS):
                ok, dr = geometry(rq, rk)
                blocks.append(toe[:, dr] if ok else masked)
            rows.append(jnp.concatenate(blocks, axis=2))
        tables.append(jnp.concatenate(rows, axis=1))
    return jnp.stack(tables, axis=1)


def _neighbourhood_attention(qkv, bias):
    t = qkv.shape[0]
    batch = t // SEQ
    hd = NA_HEAD_DIM
    return pl.pallas_call(
        _na_kernel,
        grid=(NA_HEADS, batch),
        in_specs=[
            pl.BlockSpec((SEQ, hd), lambda h, b: (b, h)),
            pl.BlockSpec((SEQ, hd), lambda h, b: (b, NA_HEADS + h)),
            pl.BlockSpec((SEQ, hd), lambda h, b: (b, 2 * NA_HEADS + h)),
            pl.BlockSpec((None, 3, NA_QTOK, NA_SLAB), lambda h, b: (h, 0, 0, 0)),
        ],
        out_specs=pl.BlockSpec((SEQ, hd), lambda h, b: (b, h)),
        out_shape=jax.ShapeDtypeStruct((t, D_MODEL), BF16),
        scratch_shapes=[pltpu.VMEM((2, NA_QTOK, NA_SLAB), F32)],
        compiler_params=_params(("arbitrary", "arbitrary")),
        name="neighbourhood_attention",
    )(qkv, qkv, qkv, bias)


def _trunk(x, p):
    for i in range(DEPTH):
        x = _ffn(x, p["norm_ffn1"][i], p["ffn1_w_in"], p["ffn1_w_out"], i)
        j = i // 2
        if i % 2 == 0:
            qk = _proj(x, p["norm_mix"][i], p["ret_w_in"], j, (0, 2 * D_MODEL), rotary=p["rotary"])
            v = _proj(x, p["norm_mix"][i], p["ret_w_in"], j, (2 * D_MODEL, 2 * D_MODEL + RET_VDIM))
            gates = _proj(x, p["norm_mix"][i], p["ret_w_in"], j, (2 * D_MODEL + RET_VDIM, RET_IN),
                          gate_gain=jnp.concatenate([p["ret_gn_f"][j], p["ret_gn_b"][j]]))
            tabs, dec = p["ret_tables"][j]
            mixed = _retention_scan(qk, v, gates, tabs, dec)
            x = _out_proj(mixed, p["ret_w_out"], j, x)
        else:
            qkv = _proj(x, p["norm_mix"][i], p["na_w_in"], j, (0, 3 * D_MODEL))
            att = _neighbourhood_attention(qkv, p["na_bias"][j])
            x = _out_proj(att, p["na_w_out"], j, x)
        x = _ffn(x, p["norm_ffn2"][i], p["ffn2_w_in"], p["ffn2_w_out"], i,
                 final_gain=p["norm_final"] if i == DEPTH - 1 else None)
    return x


def kernel(x_prompt, x_sample, norm_ffn1, ffn1_w_in, ffn1_w_out, norm_mix, norm_ffn2, ffn2_w_in, ffn2_w_out,
           ret_w_in, ret_w_out, ret_decay_f, ret_decay_b, ret_gn_f, ret_gn_b, na_w_in, na_w_out, na_rpb,
           norm_final):
    n_ret = ret_w_in.shape[0]
    n_na = na_w_in.shape[0]
    bf = lambda w: w.astype(BF16)
    p = dict(
        norm_ffn1=norm_ffn1, norm_mix=norm_mix, norm_ffn2=norm_ffn2, norm_final=norm_final,
        ffn1_w_in=bf(ffn1_w_in), ffn1_w_out=bf(ffn1_w_out), ffn2_w_in=bf(ffn2_w_in), ffn2_w_out=bf(ffn2_w_out),
        ret_w_in=bf(ret_w_in), ret_w_out=bf(ret_w_out), ret_gn_f=ret_gn_f, ret_gn_b=ret_gn_b,
        ret_tables=[_retention_tables(ret_decay_f[j], ret_decay_b[j]) for j in range(n_ret)],
        rotary=_rotary_tables(),
        na_w_in=bf(na_w_in), na_w_out=bf(na_w_out),
        na_bias=[_na_bias_tables(na_rpb[j]) for j in range(n_na)],
    )
    outs = []
    for x in (x_prompt, x_sample):
        y = _trunk(x.reshape(-1, D_MODEL), p)
        outs.append(y.reshape(x.shape))
    return tuple(outs)
```

```python
import functools

import numpy as np
import jax
import jax.numpy as jnp
from jax import lax
from jax.experimental import pallas as pl
from jax.experimental.pallas import tpu as pltpu

D_MODEL = 2048
SEQ = 4096
DEPTH = 2
GRID_W = 64
D_FF = 5632
RET_HEADS = 8
RET_DK = D_MODEL // RET_HEADS
RET_DV = 2 * RET_DK
RET_VDIM = RET_HEADS * RET_DV
RET_IN = 2 * D_MODEL + 3 * RET_VDIM
ROPE_BASE = 10000.0
NA_HEADS = 16
NA_HEAD_DIM = D_MODEL // NA_HEADS
NA_KH = 8
NA_KW = 16
RMS_EPS = 1e-6
MASK_VALUE = -1e30
LOG2_E = 1.4426950408889634

F32 = jnp.float32
BF16 = jnp.bfloat16

V7X_VMEM_BYTES = 64 * 1024 * 1024
VMEM_LIMIT_BYTES = V7X_VMEM_BYTES - 8 * 1024 * 1024

FFN_TM, FFN_TF = 1024, 512
PROJ_TM, PROJ_TN = 1024, 2048
OUT_TM = 512
SCAN_TC = 2048
SCAN_CHUNK = 256
GRID_ROWS = SEQ // GRID_W


def _params(semantics):
    return pltpu.CompilerParams(dimension_semantics=semantics, vmem_limit_bytes=VMEM_LIMIT_BYTES)


def _rmsnorm(xf, g):
    ms = jnp.mean(xf * xf, axis=-1, keepdims=True)
    return xf * lax.rsqrt(ms + RMS_EPS) * g


def _silu(g):
    return g / (1.0 + jnp.exp(-g))


def _ffn_kernel(x_ref, g_ref, wg_ref, wu_ref, wo_ref, *rest, final_norm):
    if final_norm:
        gfin_ref, o_ref, h_ref = rest
    else:
        o_ref, h_ref = rest
    j = pl.program_id(1)

    @pl.when(j == 0)
    def _():
        x = x_ref[...]
        h_ref[...] = _rmsnorm(x, g_ref[...]).astype(BF16)
        o_ref[...] = x

    h = h_ref[...]
    g = jnp.dot(h, wg_ref[...], preferred_element_type=F32)
    u = jnp.dot(h, wu_ref[...], preferred_element_type=F32)
    a = (_silu(g) * u * 0.5).astype(BF16)
    o_ref[...] += jnp.dot(a, wo_ref[...], preferred_element_type=F32)

    if final_norm:
        @pl.when(j == pl.num_programs(1) - 1)
        def _():
            o_ref[...] = _rmsnorm(o_ref[...], gfin_ref[...])


def _ffn(x, gain, w_in, w_out, layer, final_gain=None):
    t = x.shape[0]
    nj = D_FF // FFN_TF
    in_specs = [
        pl.BlockSpec((FFN_TM, D_MODEL), lambda i, j: (i, 0)),
        pl.BlockSpec((1, D_MODEL), lambda i, j: (0, 0)),
        pl.BlockSpec((None, D_MODEL, FFN_TF), lambda i, j: (layer, 0, j)),
        pl.BlockSpec((None, D_MODEL, FFN_TF), lambda i, j: (layer, 0, nj + j)),
        pl.BlockSpec((None, FFN_TF, D_MODEL), lambda i, j: (layer, j, 0)),
    ]
    args = [x, gain.reshape(1, D_MODEL), w_in, w_in, w_out]
    if final_gain is not None:
        in_specs.append(pl.BlockSpec((1, D_MODEL), lambda i, j: (0, 0)))
        args.append(final_gain.reshape(1, D_MODEL))
    return pl.pallas_call(
        functools.partial(_ffn_kernel, final_norm=final_gain is not None),
        grid=(t // FFN_TM, nj),
        in_specs=in_specs,
        out_specs=pl.BlockSpec((FFN_TM, D_MODEL), lambda i, j: (i, 0)),
        out_shape=jax.ShapeDtypeStruct((t, D_MODEL), F32),
        scratch_shapes=[pltpu.VMEM((FFN_TM, D_MODEL), BF16)],
        compiler_params=_params(("parallel", "arbitrary")),
        name="swiglu_ffn",
    )(*args)


def _proj_kernel(*refs, epilogue, h_mode):
    refs = list(refs)
    if h_mode == "input":
        h_ref = refs.pop(0)
    else:
        x_ref, g_ref = refs.pop(0), refs.pop(0)
    w_ref = refs.pop(0)
    if epilogue == "rotary":
        cos_ref, sin_ref = refs.pop(0), refs.pop(0)
    elif epilogue == "gate":
        gn_ref = refs.pop(0)
    o_ref = refs.pop(0)
    if h_mode != "input":
        h_ref = refs.pop(0)
    n = pl.program_id(1)

    if h_mode != "input":
        @pl.when(n == 0)
        def _():
            h_ref[...] = _rmsnorm(x_ref[...], g_ref[...]).astype(BF16)

    y = jnp.dot(h_ref[...], w_ref[...], preferred_element_type=F32)
    if epilogue == "plain":
        o_ref[...] = y.astype(BF16)
        return
    if epilogue == "gate":
        o_ref[...] = (_silu(y) * gn_ref[...]).astype(BF16)
        return
    cos = cos_ref[...]
    sin = sin_ref[...]
    scale = jnp.where(n < pl.num_programs(1) // 2, 1.0, RET_DK ** -0.5).astype(F32)
    half = RET_DK // 2
    for hh in range(PROJ_TN // RET_DK):
        x1 = y[:, hh * RET_DK: hh * RET_DK + half]
        x2 = y[:, hh * RET_DK + half: (hh + 1) * RET_DK]
        o_ref[:, hh * RET_DK: hh * RET_DK + half] = ((x1 * cos - x2 * sin) * scale).astype(BF16)
        o_ref[:, hh * RET_DK + half: (hh + 1) * RET_DK] = ((x1 * sin + x2 * cos) * scale).astype(BF16)


def _proj(x, gain, w, layer, cols, rotary=None, gate_gain=None, emit_h=False):
    t = x.shape[0]
    n_out = cols[1] - cols[0]
    col0 = cols[0] // PROJ_TN
    row_spec = pl.BlockSpec((PROJ_TM, D_MODEL), lambda i, n: (i, 0))
    h_mode = "input" if gain is None else ("emit" if emit_h else "scratch")
    in_specs = [row_spec]
    args = [x]
    if gain is not None:
        in_specs.append(pl.BlockSpec((1, D_MODEL), lambda i, n: (0, 0)))
        args.append(gain.reshape(1, D_MODEL))
    in_specs.append(pl.BlockSpec((None, D_MODEL, PROJ_TN), lambda i, n: (layer, 0, col0 + n)))
    args.append(w)
    epilogue = "plain"
    if rotary is not None:
        epilogue = "rotary"
        seq_tiles = SEQ // PROJ_TM
        tab_spec = pl.BlockSpec((PROJ_TM, RET_DK // 2), lambda i, n: (i % seq_tiles, 0))
        in_specs += [tab_spec, tab_spec]
        args += list(rotary)
    elif gate_gain is not None:
        epilogue = "gate"
        in_specs.append(pl.BlockSpec((1, PROJ_TN), lambda i, n: (0, n)))
        args.append(gate_gain.reshape(1, n_out).astype(F32))
    out_specs = pl.BlockSpec((PROJ_TM, PROJ_TN), lambda i, n: (i, n))
    out_shape = jax.ShapeDtypeStruct((t, n_out), BF16)
    if h_mode == "emit":
        out_specs = [out_specs, row_spec]
        out_shape = [out_shape, jax.ShapeDtypeStruct((t, D_MODEL), BF16)]
    return pl.pallas_call(
        functools.partial(_proj_kernel, epilogue=epilogue, h_mode=h_mode),
        grid=(t // PROJ_TM, n_out // PROJ_TN),
        in_specs=in_specs,
        out_specs=out_specs,
        out_shape=out_shape,
        scratch_shapes=[pltpu.VMEM((PROJ_TM, D_MODEL), BF16)] if h_mode == "scratch" else [],
        compiler_params=_params(("parallel", "arbitrary")),
        name="proj_" + epilogue + "_h" + h_mode,
    )(*args)


def _out_kernel(a_ref, w_ref, x_ref, o_ref):
    o_ref[...] = x_ref[...] + jnp.dot(a_ref[...], w_ref[...], preferred_element_type=F32)


def _out_proj(a, w, layer, x):
    t, k = a.shape
    return pl.pallas_call(
        _out_kernel,
        grid=(t // OUT_TM,),
        in_specs=[
            pl.BlockSpec((OUT_TM, k), lambda i: (i, 0)),
            pl.BlockSpec((None, k, D_MODEL), lambda i: (layer, 0, 0), pipeline_mode=pl.Buffered(1)),
            pl.BlockSpec((OUT_TM, D_MODEL), lambda i: (i, 0)),
        ],
        out_specs=pl.BlockSpec((OUT_TM, D_MODEL), lambda i: (i, 0)),
        out_shape=jax.ShapeDtypeStruct((t, D_MODEL), F32),
        compiler_params=_params(("parallel",)),
        name="out_proj_residual",
    )(a, w, x)


SCAN_NBLK = SEQ // SCAN_TC
SCAN_CHUNKS = SCAN_TC // SCAN_CHUNK


def _scan_kernel(dec_ref, q_ref, k_ref, v_ref, gf_ref, gb_ref, tab_ref, o_ref,
                 r_ref, ob_ref, qs_ref, ks_ref, vs_ref):
    h = pl.program_id(1)
    t = pl.program_id(2)
    c_len = SCAN_CHUNK

    @pl.when((t == 0) | (t == SCAN_NBLK))
    def _():
        r_ref[...] = jnp.zeros_like(r_ref)

    def run(backward):
        inner = tab_ref[0]
        cross = jnp.concatenate([tab_ref[1]] * (RET_DV // c_len), axis=1)
        into = jnp.concatenate([tab_ref[2]] * (RET_DK // c_len), axis=1)
        dec = dec_ref[1 if backward else 0, h]
        gate_ref = gb_ref if backward else gf_ref
        blk = (SCAN_NBLK - 1 - t) if backward else (t - SCAN_NBLK)
        order = range(SCAN_CHUNKS - 1, -1, -1) if backward else range(SCAN_CHUNKS)
        for c in order:
            rows = pl.ds(c * c_len, c_len)
            seq_rows = pl.ds(pl.multiple_of(blk * SCAN_TC + c * c_len, c_len), c_len)
            if backward:
                q = q_ref[rows, :]
                k = k_ref[rows, :]
                v = v_ref[rows, :]
                qs_ref[seq_rows, :] = q
                ks_ref[seq_rows, :] = k
                vs_ref[seq_rows, :] = v
            else:
                q = qs_ref[seq_rows, :]
                k = ks_ref[seq_rows, :]
                v = vs_ref[seq_rows, :]
            s = lax.dot_general(q, k, (((1,), (1,)), ((), ())), preferred_element_type=F32) * inner
            kd = (k.astype(F32) * into).astype(BF16)
            r = r_ref[...]
            sv = jnp.dot(jnp.concatenate([s.astype(BF16), kd.T], axis=0), v, preferred_element_type=F32)
            o = sv[:c_len] + jnp.dot(q, r.astype(BF16), preferred_element_type=F32) * cross
            r_ref[...] = r * dec + sv[c_len:]
            on = o * lax.rsqrt(jnp.mean(o * o, axis=-1, keepdims=True) + RMS_EPS)
            m = gate_ref[rows, :].astype(F32) * on
            if backward:
                ob_ref[seq_rows, :] = m
            else:
                o_ref[rows, :] = (m + ob_ref[seq_rows, :]).astype(BF16)

    @pl.when(t < SCAN_NBLK)
    def _():
        o_ref[...] = jnp.zeros_like(o_ref)
        run(True)

    @pl.when(t >= SCAN_NBLK)
    def _():
        run(False)


def _retention_scan(qk, v, gates, tabs, dec):
    t = qk.shape[0]
    batch = t // SEQ
    nb = SCAN_NBLK

    def bwd_blk(s):
        return jnp.where(s < nb, nb - 1 - s, 0)

    def fwd_blk(s):
        return jnp.where(s < nb, 0, s - nb)

    nh = RET_HEADS
    in_specs = [
        pl.BlockSpec(memory_space=pltpu.SMEM),
        pl.BlockSpec((SCAN_TC, RET_DK), lambda b, h, s: (b * nb + bwd_blk(s), h)),
        pl.BlockSpec((SCAN_TC, RET_DK), lambda b, h, s: (b * nb + bwd_blk(s), nh + h)),
        pl.BlockSpec((SCAN_TC, RET_DV), lambda b, h, s: (b * nb + bwd_blk(s), h)),
        pl.BlockSpec((SCAN_TC, RET_DV), lambda b, h, s: (b * nb + fwd_blk(s), h)),
        pl.BlockSpec((SCAN_TC, RET_DV), lambda b, h, s: (b * nb + bwd_blk(s), nh + h)),
        pl.BlockSpec((None, None, 3, SCAN_CHUNK, SCAN_CHUNK),
                     lambda b, h, s: (jnp.where(s < nb, 1, 0), h, 0, 0, 0)),
    ]
    return pl.pallas_call(
        _scan_kernel,
        grid=(batch, RET_HEADS, 2 * nb),
        in_specs=in_specs,
        out_specs=pl.BlockSpec((SCAN_TC, RET_DV), lambda b, h, s: (b * nb + fwd_blk(s), h)),
        out_shape=jax.ShapeDtypeStruct((t, RET_VDIM), BF16),
        scratch_shapes=[pltpu.VMEM((RET_DK, RET_DV), F32), pltpu.VMEM((SEQ, RET_DV), F32),
                        pltpu.VMEM((SEQ, RET_DK), BF16), pltpu.VMEM((SEQ, RET_DK), BF16),
                        pltpu.VMEM((SEQ, RET_DV), BF16)],
        compiler_params=_params(("arbitrary", "arbitrary", "arbitrary")),
        name="retention_scan",
    )(dec, qk, qk, v, gates, gates, tabs)


def _retention_tables(decay_f, decay_b):
    c = SCAN_CHUNK
    pos = jnp.arange(c, dtype=F32)
    ones = jnp.ones((1, 1, c), F32)

    def tables(log_gamma, backward):
        lg = log_gamma[:, None, None]
        diff = pos[:, None] - pos[None, :]
        if backward:
            diff = -diff
        inner = jnp.where(diff[None] >= 0, jnp.exp(lg * jnp.maximum(diff, 0.0)[None]), 0.0)
        cross_pos = (c - pos) if backward else (pos + 1.0)
        into_pos = pos if backward else (c - 1.0 - pos)
        cross = jnp.exp(log_gamma[:, None] * cross_pos[None])[:, :, None] * ones
        into = jnp.exp(log_gamma[:, None] * into_pos[None])[:, :, None] * ones
        return jnp.stack([inner, cross, into], axis=1), jnp.exp(log_gamma * c)

    lg_f = jnp.log1p(-jnp.exp(decay_f.astype(F32)))
    lg_b = jnp.log1p(-jnp.exp(decay_b.astype(F32)))
    tab_f, dec_f = tables(lg_f, False)
    tab_b, dec_b = tables(lg_b, True)
    return jnp.stack([tab_f, tab_b]), jnp.stack([dec_f, dec_b])


def _rotary_tables():
    d = RET_DK
    inv = ROPE_BASE ** (-jnp.arange(0, d, 2, dtype=F32) / d)
    ang = jnp.arange(SEQ, dtype=F32)[:, None] * inv[None, :]
    return jnp.cos(ang), jnp.sin(ang)


NA_QROWS = 4
NA_GROUPS = GRID_ROWS // NA_QROWS
NA_SLAB_ROWS = NA_QROWS + NA_KH
NA_QTOK = NA_QROWS * GRID_W
NA_SLAB = NA_SLAB_ROWS * GRID_W


def _na_kernel(q_ref, k_ref, v_ref, bias_ref, o_ref, s_ref):
    scale = NA_HEAD_DIM ** -0.5 * LOG2_E

    def windows(g):
        s0 = jnp.clip(NA_QROWS * g - NA_KH // 2, 0, GRID_ROWS - NA_SLAB_ROWS)
        rows = pl.ds(pl.multiple_of(g * NA_QTOK, NA_QTOK), NA_QTOK)
        slab = pl.ds(pl.multiple_of(s0 * GRID_W, NA_QTOK), NA_SLAB)
        return rows, slab

    def scores(g, slot):
        rows, slab = windows(g)
        kind = jnp.where(g == 0, 0, jnp.where(g == NA_GROUPS - 1, 2, 1))
        s = lax.dot_general(q_ref[rows, :], k_ref[slab, :], (((1,), (1,)), ((), ())),
                            preferred_element_type=F32)
        s_ref[slot] = s * scale + bias_ref[kind]

    def attend(g, slot):
        rows, slab = windows(g)
        s = s_ref[slot]
        e = jnp.exp2(s - jnp.max(s, axis=-1, keepdims=True))
        o = jnp.dot(e.astype(BF16), v_ref[slab, :], preferred_element_type=F32)
        o_ref[rows, :] = (o * (1.0 / jnp.sum(e, axis=-1, keepdims=True))).astype(BF16)

    scores(0, 0)

    def pair(i, carry):
        g = 2 * i
        scores(g + 1, 1)
        attend(g, 0)
        scores(g + 2, 0)
        attend(g + 1, 1)
        return carry

    lax.fori_loop(0, NA_GROUPS // 2 - 1, pair, 0, unroll=True)
    scores(NA_GROUPS - 1, 1)
    attend(NA_GROUPS - 2, 0)
    attend(NA_GROUPS - 1, 1)


def _na_bias_tables(rpb):
    pad = GRID_W - NA_KW
    padded = jnp.pad(rpb.astype(F32), ((0, 0), (0, 0), (pad, pad)))
    toe = jnp.stack([padded[:, :, GRID_W - 1 - c: 2 * GRID_W - 1 - c] for c in range(GRID_W)], axis=2)
    cq = np.arange(GRID_W)
    ck = np.arange(GRID_W)
    qwin = np.clip(cq - NA_KW // 2, 0, GRID_W - NA_KW)
    col_ok = (ck[None, :] >= qwin[:, None]) & (ck[None, :] < qwin[:, None] + NA_KW)
    toe = jnp.where(col_ok[None, None], toe * LOG2_E, MASK_VALUE)
    masked = jnp.full((NA_HEADS, GRID_W, GRID_W), MASK_VALUE, F32)

    geometries = (
        lambda rq, rk: (rk < NA_KH, rk - rq + NA_KH - 1),
        lambda rq, rk: (rq <= rk < rq + NA_KH, rk - rq + NA_KH // 2 - 1),
        lambda rq, rk: (rk >= NA_QROWS, rk - rq - 1),
    )
    tables = []
    for geometry in geometries:
        rows = []
        for rq in range(NA_QROWS):
            blocks = []
            for rk in range(NA_SLAB_ROWS):
                ok, dr = geometry(rq, rk)
                blocks.append(toe[:, dr] if ok else masked)
            rows.append(jnp.concatenate(blocks, axis=2))
        tables.append(jnp.concatenate(rows, axis=1))
    return jnp.stack(tables, axis=1)


def _neighbourhood_attention(qkv, bias):
    t = qkv.shape[0]
    batch = t // SEQ
    hd = NA_HEAD_DIM
    return pl.pallas_call(
        _na_kernel,
        grid=(NA_HEADS, batch),
        in_specs=[
            pl.BlockSpec((SEQ, hd), lambda h, b: (b, h)),
            pl.BlockSpec((SEQ, hd), lambda h, b: (b, NA_HEADS + h)),
            pl.BlockSpec((SEQ, hd), lambda h, b: (b, 2 * NA_HEADS + h)),
            pl.BlockSpec((None, 3, NA_QTOK, NA_SLAB), lambda h, b: (h, 0, 0, 0)),
        ],
        out_specs=pl.BlockSpec((SEQ, hd), lambda h, b: (b, h)),
        out_shape=jax.ShapeDtypeStruct((t, D_MODEL), BF16),
        scratch_shapes=[pltpu.VMEM((2, NA_QTOK, NA_SLAB), F32)],
        compiler_params=_params(("arbitrary", "arbitrary")),
        name="neighbourhood_attention",
    )(qkv, qkv, qkv, bias)


def _trunk(x, p):
    for i in range(DEPTH):
        x = _ffn(x, p["norm_ffn1"][i], p["ffn1_w_in"], p["ffn1_w_out"], i)
        j = i // 2
        if i % 2 == 0:
            qk, h = _proj(x, p["norm_mix"][i], p["ret_w_in"], j, (0, 2 * D_MODEL), rotary=p["rotary"],
                          emit_h=True)
            v = _proj(h, None, p["ret_w_in"], j, (2 * D_MODEL, 2 * D_MODEL + RET_VDIM))
            gates = _proj(h, None, p["ret_w_in"], j, (2 * D_MODEL + RET_VDIM, RET_IN),
                          gate_gain=jnp.concatenate([p["ret_gn_f"][j], p["ret_gn_b"][j]]))
            tabs, dec = p["ret_tables"][j]
            mixed = _retention_scan(qk, v, gates, tabs, dec)
            x = _out_proj(mixed, p["ret_w_out"], j, x)
        else:
            qkv = _proj(x, p["norm_mix"][i], p["na_w_in"], j, (0, 3 * D_MODEL))
            att = _neighbourhood_attention(qkv, p["na_bias"][j])
            x = _out_proj(att, p["na_w_out"], j, x)
        x = _ffn(x, p["norm_ffn2"][i], p["ffn2_w_in"], p["ffn2_w_out"], i,
                 final_gain=p["norm_final"] if i == DEPTH - 1 else None)
    return x


def kernel(x_prompt, x_sample, norm_ffn1, ffn1_w_in, ffn1_w_out, norm_mix, norm_ffn2, ffn2_w_in, ffn2_w_out,
           ret_w_in, ret_w_out, ret_decay_f, ret_decay_b, ret_gn_f, ret_gn_b, na_w_in, na_w_out, na_rpb,
           norm_final):
    n_ret = ret_w_in.shape[0]
    n_na = na_w_in.shape[0]
    bf = lambda w: w.astype(BF16)
    p = dict(
        norm_ffn1=norm_ffn1, norm_mix=norm_mix, norm_ffn2=norm_ffn2, norm_final=norm_final,
        ffn1_w_in=bf(ffn1_w_in), ffn1_w_out=bf(ffn1_w_out), ffn2_w_in=bf(ffn2_w_in), ffn2_w_out=bf(ffn2_w_out),
        ret_w_in=bf(ret_w_in), ret_w_out=bf(ret_w_out), ret_gn_f=ret_gn_f, ret_gn_b=ret_gn_b,
        ret_tables=[_retention_tables(ret_decay_f[j], ret_decay_b[j]) for j in range(n_ret)],
        rotary=_rotary_tables(),
        na_w_in=bf(na_w_in), na_w_out=bf(na_w_out),
        na_bias=[_na_bias_tables(na_rpb[j]) for j in range(n_na)],
    )
    outs = []
    for x in (x_prompt, x_sample):
        y = _trunk(x.reshape(-1, D_MODEL), p)
        outs.append(y.reshape(x.shape))
    return tuple(outs)
```

```python
import functools

import numpy as np
import jax
import jax.numpy as jnp
from jax import lax
from jax.experimental import pallas as pl
from jax.experimental.pallas import tpu as pltpu

D_MODEL = 2048
SEQ = 4096
DEPTH = 2
GRID_W = 64
D_FF = 5632
RET_HEADS = 8
RET_DK = D_MODEL // RET_HEADS
RET_DV = 2 * RET_DK
RET_VDIM = RET_HEADS * RET_DV
RET_IN = 2 * D_MODEL + 3 * RET_VDIM
ROPE_BASE = 10000.0
NA_HEADS = 16
NA_HEAD_DIM = D_MODEL // NA_HEADS
NA_KH = 8
NA_KW = 16
RMS_EPS = 1e-6
MASK_VALUE = -1e30
LOG2_E = 1.4426950408889634

F32 = jnp.float32
BF16 = jnp.bfloat16

V7X_VMEM_BYTES = 64 * 1024 * 1024
VMEM_LIMIT_BYTES = V7X_VMEM_BYTES - 8 * 1024 * 1024

FFN_TM, FFN_TF = 1024, 512
PROJ_TM, PROJ_TN = 1024, 2048
OUT_TM = 512
SCAN_TC = 2048
SCAN_CHUNK = 256
GRID_ROWS = SEQ // GRID_W


def _params(semantics):
    return pltpu.CompilerParams(dimension_semantics=semantics, vmem_limit_bytes=VMEM_LIMIT_BYTES)


def _rmsnorm(xf, g):
    ms = jnp.mean(xf * xf, axis=-1, keepdims=True)
    return xf * lax.rsqrt(ms + RMS_EPS) * g


def _silu(g):
    return g / (1.0 + jnp.exp(-g))


def _ffn_kernel(x_ref, g_ref, wg_ref, wu_ref, wo_ref, *rest, final_norm):
    if final_norm:
        gfin_ref, o_ref, h_ref = rest
    else:
        o_ref, h_ref = rest
    j = pl.program_id(1)

    @pl.when(j == 0)
    def _():
        x = x_ref[...]
        h_ref[...] = _rmsnorm(x, g_ref[...]).astype(BF16)
        o_ref[...] = x

    h = h_ref[...]
    g = jnp.dot(h, wg_ref[...], preferred_element_type=F32)
    u = jnp.dot(h, wu_ref[...], preferred_element_type=F32)
    a = (_silu(g) * u * 0.5).astype(BF16)
    o_ref[...] += jnp.dot(a, wo_ref[...], preferred_element_type=F32)

    if final_norm:
        @pl.when(j == pl.num_programs(1) - 1)
        def _():
            o_ref[...] = _rmsnorm(o_ref[...], gfin_ref[...])


def _ffn(x, gain, w_in, w_out, layer, final_gain=None):
    t = x.shape[0]
    nj = D_FF // FFN_TF
    in_specs = [
        pl.BlockSpec((FFN_TM, D_MODEL), lambda i, j: (i, 0)),
        pl.BlockSpec((1, D_MODEL), lambda i, j: (0, 0)),
        pl.BlockSpec((None, D_MODEL, FFN_TF), lambda i, j: (layer, 0, j)),
        pl.BlockSpec((None, D_MODEL, FFN_TF), lambda i, j: (layer, 0, nj + j)),
        pl.BlockSpec((None, FFN_TF, D_MODEL), lambda i, j: (layer, j, 0)),
    ]
    args = [x, gain.reshape(1, D_MODEL), w_in, w_in, w_out]
    if final_gain is not None:
        in_specs.append(pl.BlockSpec((1, D_MODEL), lambda i, j: (0, 0)))
        args.append(final_gain.reshape(1, D_MODEL))
    return pl.pallas_call(
        functools.partial(_ffn_kernel, final_norm=final_gain is not None),
        grid=(t // FFN_TM, nj),
        in_specs=in_specs,
        out_specs=pl.BlockSpec((FFN_TM, D_MODEL), lambda i, j: (i, 0)),
        out_shape=jax.ShapeDtypeStruct((t, D_MODEL), F32),
        scratch_shapes=[pltpu.VMEM((FFN_TM, D_MODEL), BF16)],
        compiler_params=_params(("parallel", "arbitrary")),
        name="swiglu_ffn",
    )(*args)


def _proj_kernel(*refs, epilogue, h_mode):
    refs = list(refs)
    if h_mode == "input":
        h_ref = refs.pop(0)
    else:
        x_ref, g_ref = refs.pop(0), refs.pop(0)
    w_ref = refs.pop(0)
    if epilogue == "rotary":
        cos_ref, sin_ref = refs.pop(0), refs.pop(0)
    elif epilogue == "gate":
        gn_ref = refs.pop(0)
    o_ref = refs.pop(0)
    if h_mode != "input":
        h_ref = refs.pop(0)
    n = pl.program_id(1)

    if h_mode != "input":
        @pl.when(n == 0)
        def _():
            h_ref[...] = _rmsnorm(x_ref[...], g_ref[...]).astype(BF16)

    y = jnp.dot(h_ref[...], w_ref[...], preferred_element_type=F32)
    if epilogue == "plain":
        o_ref[...] = y.astype(BF16)
        return
    if epilogue == "gate":
        o_ref[...] = (_silu(y) * gn_ref[...]).astype(BF16)
        return
    cos = cos_ref[...]
    sin = sin_ref[...]
    scale = jnp.where(n < pl.num_programs(1) // 2, 1.0, RET_DK ** -0.5).astype(F32)
    half = RET_DK // 2
    for hh in range(PROJ_TN // RET_DK):
        x1 = y[:, hh * RET_DK: hh * RET_DK + half]
        x2 = y[:, hh * RET_DK + half: (hh + 1) * RET_DK]
        o_ref[:, hh * RET_DK: hh * RET_DK + half] = ((x1 * cos - x2 * sin) * scale).astype(BF16)
        o_ref[:, hh * RET_DK + half: (hh + 1) * RET_DK] = ((x1 * sin + x2 * cos) * scale).astype(BF16)


def _proj(x, gain, w, layer, cols, rotary=None, gate_gain=None, emit_h=False):
    t = x.shape[0]
    n_out = cols[1] - cols[0]
    col0 = cols[0] // PROJ_TN
    row_spec = pl.BlockSpec((PROJ_TM, D_MODEL), lambda i, n: (i, 0))
    h_mode = "input" if gain is None else ("emit" if emit_h else "scratch")
    in_specs = [row_spec]
    args = [x]
    if gain is not None:
        in_specs.append(pl.BlockSpec((1, D_MODEL), lambda i, n: (0, 0)))
        args.append(gain.reshape(1, D_MODEL))
    in_specs.append(pl.BlockSpec((None, D_MODEL, PROJ_TN), lambda i, n: (layer, 0, col0 + n)))
    args.append(w)
    epilogue = "plain"
    if rotary is not None:
        epilogue = "rotary"
        seq_tiles = SEQ // PROJ_TM
        tab_spec = pl.BlockSpec((PROJ_TM, RET_DK // 2), lambda i, n: (i % seq_tiles, 0))
        in_specs += [tab_spec, tab_spec]
        args += list(rotary)
    elif gate_gain is not None:
        epilogue = "gate"
        in_specs.append(pl.BlockSpec((1, PROJ_TN), lambda i, n: (0, n)))
        args.append(gate_gain.reshape(1, n_out).astype(F32))
    out_specs = pl.BlockSpec((PROJ_TM, PROJ_TN), lambda i, n: (i, n))
    out_shape = jax.ShapeDtypeStruct((t, n_out), BF16)
    if h_mode == "emit":
        out_specs = [out_specs, row_spec]
        out_shape = [out_shape, jax.ShapeDtypeStruct((t, D_MODEL), BF16)]
    return pl.pallas_call(
        functools.partial(_proj_kernel, epilogue=epilogue, h_mode=h_mode),
        grid=(t // PROJ_TM, n_out // PROJ_TN),
        in_specs=in_specs,
        out_specs=out_specs,
        out_shape=out_shape,
        scratch_shapes=[pltpu.VMEM((PROJ_TM, D_MODEL), BF16)] if h_mode == "scratch" else [],
        compiler_params=_params(("parallel", "arbitrary")),
        name="proj_" + epilogue + "_h" + h_mode,
    )(*args)


def _out_kernel(a_ref, w_ref, x_ref, o_ref):
    o_ref[...] = x_ref[...] + jnp.dot(a_ref[...], w_ref[...], preferred_element_type=F32)


def _out_proj(a, w, layer, x):
    t, k = a.shape
    return pl.pallas_call(
        _out_kernel,
        grid=(t // OUT_TM,),
        in_specs=[
            pl.BlockSpec((OUT_TM, k), lambda i: (i, 0)),
            pl.BlockSpec((None, k, D_MODEL), lambda i: (layer, 0, 0), pipeline_mode=pl.Buffered(1)),
            pl.BlockSpec((OUT_TM, D_MODEL), lambda i: (i, 0)),
        ],
        out_specs=pl.BlockSpec((OUT_TM, D_MODEL), lambda i: (i, 0)),
        out_shape=jax.ShapeDtypeStruct((t, D_MODEL), F32),
        compiler_params=_params(("parallel",)),
        name="out_proj_residual",
    )(a, w, x)


SCAN_NBLK = SEQ // SCAN_TC
SCAN_CHUNKS = SCAN_TC // SCAN_CHUNK


def _scan_kernel(dec_ref, q_ref, k_ref, v_ref, gf_ref, gb_ref, tab_ref, o_ref, r_ref, ob_ref):
    h = pl.program_id(1)
    t = pl.program_id(2)
    c_len = SCAN_CHUNK

    @pl.when((t == 0) | (t == SCAN_NBLK))
    def _():
        r_ref[...] = jnp.zeros_like(r_ref)

    def run(backward):
        inner = tab_ref[0]
        cross = jnp.concatenate([tab_ref[1]] * (RET_DV // c_len), axis=1)
        into = jnp.concatenate([tab_ref[2]] * (RET_DK // c_len), axis=1)
        dec = dec_ref[1 if backward else 0, h]
        gate_ref = gb_ref if backward else gf_ref
        blk = (SCAN_NBLK - 1 - t) if backward else (t - SCAN_NBLK)
        order = range(SCAN_CHUNKS - 1, -1, -1) if backward else range(SCAN_CHUNKS)
        for c in order:
            rows = pl.ds(c * c_len, c_len)
            seq_rows = pl.ds(pl.multiple_of(blk * SCAN_TC + c * c_len, c_len), c_len)
            q = q_ref[rows, :]
            k = k_ref[rows, :]
            v = v_ref[rows, :]
            s = lax.dot_general(q, k, (((1,), (1,)), ((), ())), preferred_element_type=F32) * inner
            kd = (k.astype(F32) * into).astype(BF16)
            r = r_ref[...]
            sv = jnp.dot(jnp.concatenate([s.astype(BF16), kd.T], axis=0), v, preferred_element_type=F32)
            o = sv[:c_len] + jnp.dot(q, r.astype(BF16), preferred_element_type=F32) * cross
            r_ref[...] = r * dec + sv[c_len:]
            on = o * lax.rsqrt(jnp.mean(o * o, axis=-1, keepdims=True) + RMS_EPS)
            m = gate_ref[rows, :].astype(F32) * on
            if backward:
                ob_ref[seq_rows, :] = m
            else:
                o_ref[rows, :] = (m + ob_ref[seq_rows, :]).astype(BF16)

    @pl.when(t < SCAN_NBLK)
    def _():
        o_ref[...] = jnp.zeros_like(o_ref)
        run(True)

    @pl.when(t >= SCAN_NBLK)
    def _():
        run(False)


def _retention_scan(qk, v, gates, tabs, dec):
    t = qk.shape[0]
    batch = t // SEQ
    nb = SCAN_NBLK

    def bwd_blk(s):
        return jnp.where(s < nb, nb - 1 - s, 0)

    def fwd_blk(s):
        return jnp.where(s < nb, 0, s - nb)

    def cur_blk(s):
        return jnp.where(s < nb, nb - 1 - s, s - nb)

    nh = RET_HEADS
    in_specs = [
        pl.BlockSpec(memory_space=pltpu.SMEM),
        pl.BlockSpec((SCAN_TC, RET_DK), lambda b, h, s: (b * nb + cur_blk(s), h)),
        pl.BlockSpec((SCAN_TC, RET_DK), lambda b, h, s: (b * nb + cur_blk(s), nh + h)),
        pl.BlockSpec((SCAN_TC, RET_DV), lambda b, h, s: (b * nb + cur_blk(s), h)),
        pl.BlockSpec((SCAN_TC, RET_DV), lambda b, h, s: (b * nb + fwd_blk(s), h)),
        pl.BlockSpec((SCAN_TC, RET_DV), lambda b, h, s: (b * nb + bwd_blk(s), nh + h)),
        pl.BlockSpec((None, None, 3, SCAN_CHUNK, SCAN_CHUNK),
                     lambda b, h, s: (jnp.where(s < nb, 1, 0), h, 0, 0, 0)),
    ]
    return pl.pallas_call(
        _scan_kernel,
        grid=(batch, RET_HEADS, 2 * nb),
        in_specs=in_specs,
        out_specs=pl.BlockSpec((SCAN_TC, RET_DV), lambda b, h, s: (b * nb + fwd_blk(s), h)),
        out_shape=jax.ShapeDtypeStruct((t, RET_VDIM), BF16),
        scratch_shapes=[pltpu.VMEM((RET_DK, RET_DV), F32), pltpu.VMEM((SEQ, RET_DV), F32)],
        compiler_params=_params(("arbitrary", "arbitrary", "arbitrary")),
        name="retention_scan",
    )(dec, qk, qk, v, gates, gates, tabs)


def _retention_tables(decay_f, decay_b):
    c = SCAN_CHUNK
    pos = jnp.arange(c, dtype=F32)
    ones = jnp.ones((1, 1, c), F32)

    def tables(log_gamma, backward):
        lg = log_gamma[:, None, None]
        diff = pos[:, None] - pos[None, :]
        if backward:
            diff = -diff
        inner = jnp.where(diff[None] >= 0, jnp.exp(lg * jnp.maximum(diff, 0.0)[None]), 0.0)
        cross_pos = (c - pos) if backward else (pos + 1.0)
        into_pos = pos if backward else (c - 1.0 - pos)
        cross = jnp.exp(log_gamma[:, None] * cross_pos[None])[:, :, None] * ones
        into = jnp.exp(log_gamma[:, None] * into_pos[None])[:, :, None] * ones
        return jnp.stack([inner, cross, into], axis=1), jnp.exp(log_gamma * c)

    lg_f = jnp.log1p(-jnp.exp(decay_f.astype(F32)))
    lg_b = jnp.log1p(-jnp.exp(decay_b.astype(F32)))
    tab_f, dec_f = tables(lg_f, False)
    tab_b, dec_b = tables(lg_b, True)
    return jnp.stack([tab_f, tab_b]), jnp.stack([dec_f, dec_b])


def _rotary_tables():
    d = RET_DK
    inv = ROPE_BASE ** (-jnp.arange(0, d, 2, dtype=F32) / d)
    ang = jnp.arange(SEQ, dtype=F32)[:, None] * inv[None, :]
    return jnp.cos(ang), jnp.sin(ang)


NA_QROWS = 4
NA_GROUPS = GRID_ROWS // NA_QROWS
NA_SLAB_ROWS = NA_QROWS + NA_KH
NA_QTOK = NA_QROWS * GRID_W
NA_SLAB = NA_SLAB_ROWS * GRID_W


NA_GEOMETRIES = (
    lambda rq, rk: (rk < NA_KH, rk - rq + NA_KH - 1),
    lambda rq, rk: (rq <= rk < rq + NA_KH, rk - rq + NA_KH // 2 - 1),
    lambda rq, rk: (rk >= NA_QROWS, rk - rq - 1),
)


def _na_kernel(q_ref, k_ref, v_ref, toe_ref, o_ref, s_ref, bias_ref):
    scale = NA_HEAD_DIM ** -0.5 * LOG2_E

    @pl.when(pl.program_id(1) == 0)
    def _():
        first_row = lax.broadcasted_iota(jnp.int32, (GRID_W, 2 * GRID_W), 1) < GRID_W
        masked = jnp.full((GRID_W, 2 * GRID_W), MASK_VALUE, F32)
        for kind, geometry in enumerate(NA_GEOMETRIES):
            for rq in range(NA_QROWS):
                for rk in range(0, NA_SLAB_ROWS, 2):
                    (ok0, dr0), (ok1, dr1) = geometry(rq, rk), geometry(rq, rk + 1)
                    tile = jnp.where(first_row, toe_ref[dr0] if ok0 else masked,
                                     toe_ref[dr1] if ok1 else masked) if (ok0 or ok1) else masked
                    bias_ref[kind, rq * GRID_W:(rq + 1) * GRID_W, rk * GRID_W:(rk + 2) * GRID_W] = tile

    def windows(g):
        s0 = jnp.clip(NA_QROWS * g - NA_KH // 2, 0, GRID_ROWS - NA_SLAB_ROWS)
        rows = pl.ds(pl.multiple_of(g * NA_QTOK, NA_QTOK), NA_QTOK)
        slab = pl.ds(pl.multiple_of(s0 * GRID_W, NA_QTOK), NA_SLAB)
        return rows, slab

    def scores(g, slot):
        rows, slab = windows(g)
        kind = jnp.where(g == 0, 0, jnp.where(g == NA_GROUPS - 1, 2, 1))
        s = lax.dot_general(q_ref[rows, :], k_ref[slab, :], (((1,), (1,)), ((), ())),
                            preferred_element_type=F32)
        s_ref[slot] = s * scale + bias_ref[kind]

    def attend(g, slot):
        rows, slab = windows(g)
        s = s_ref[slot]
        e = jnp.exp2(s - jnp.max(s, axis=-1, keepdims=True))
        o = jnp.dot(e.astype(BF16), v_ref[slab, :], preferred_element_type=F32)
        o_ref[rows, :] = (o * (1.0 / jnp.sum(e, axis=-1, keepdims=True))).astype(BF16)

    scores(0, 0)

    def pair(i, carry):
        g = 2 * i
        scores(g + 1, 1)
        attend(g, 0)
        scores(g + 2, 0)
        attend(g + 1, 1)
        return carry

    lax.fori_loop(0, NA_GROUPS // 2 - 1, pair, 0, unroll=True)
    scores(NA_GROUPS - 1, 1)
    attend(NA_GROUPS - 2, 0)
    attend(NA_GROUPS - 1, 1)


def _na_bias_tables(rpb):
    pad = GRID_W - NA_KW
    padded = jnp.pad(rpb.astype(F32), ((0, 0), (0, 0), (pad, pad)))
    toe = jnp.stack([padded[:, :, GRID_W - 1 - c: 2 * GRID_W - 1 - c] for c in range(GRID_W)], axis=2)
    cq = np.arange(GRID_W)
    ck = np.arange(GRID_W)
    qwin = np.clip(cq - NA_KW // 2, 0, GRID_W - NA_KW)
    col_ok = (ck[None, :] >= qwin[:, None]) & (ck[None, :] < qwin[:, None] + NA_KW)
    toe = jnp.where(col_ok[None, None], toe * LOG2_E, MASK_VALUE)
    return jnp.concatenate([toe, toe], axis=-1)


def _neighbourhood_attention(qkv, bias):
    t = qkv.shape[0]
    batch = t // SEQ
    hd = NA_HEAD_DIM
    return pl.pallas_call(
        _na_kernel,
        grid=(NA_HEADS, batch),
        in_specs=[
            pl.BlockSpec((SEQ, hd), lambda h, b: (b, h)),
            pl.BlockSpec((SEQ, hd), lambda h, b: (b, NA_HEADS + h)),
            pl.BlockSpec((SEQ, hd), lambda h, b: (b, 2 * NA_HEADS + h)),
            pl.BlockSpec((None, 2 * NA_KH - 1, GRID_W, 2 * GRID_W), lambda h, b: (h, 0, 0, 0)),
        ],
        out_specs=pl.BlockSpec((SEQ, hd), lambda h, b: (b, h)),
        out_shape=jax.ShapeDtypeStruct((t, D_MODEL), BF16),
        scratch_shapes=[pltpu.VMEM((2, NA_QTOK, NA_SLAB), F32), pltpu.VMEM((3, NA_QTOK, NA_SLAB), F32)],
        compiler_params=_params(("arbitrary", "arbitrary")),
        name="neighbourhood_attention",
    )(qkv, qkv, qkv, bias)


def _trunk(x, p):
    for i in range(DEPTH):
        x = _ffn(x, p["norm_ffn1"][i], p["ffn1_w_in"], p["ffn1_w_out"], i)
        j = i // 2
        if i % 2 == 0:
            qk, h = _proj(x, p["norm_mix"][i], p["ret_w_in"], j, (0, 2 * D_MODEL), rotary=p["rotary"],
                          emit_h=True)
            v = _proj(h, None, p["ret_w_in"], j, (2 * D_MODEL, 2 * D_MODEL + RET_VDIM))
            gates = _proj(h, None, p["ret_w_in"], j, (2 * D_MODEL + RET_VDIM, RET_IN),
                          gate_gain=jnp.concatenate([p["ret_gn_f"][j], p["ret_gn_b"][j]]))
            tabs, dec = p["ret_tables"][j]
            mixed = _retention_scan(qk, v, gates, tabs, dec)
            x = _out_proj(mixed, p["ret_w_out"], j, x)
        else:
            qkv = _proj(x, p["norm_mix"][i], p["na_w_in"], j, (0, 3 * D_MODEL))
            att = _neighbourhood_attention(qkv, p["na_bias"][j])
            x = _out_proj(att, p["na_w_out"], j, x)
        x = _ffn(x, p["norm_ffn2"][i], p["ffn2_w_in"], p["ffn2_w_out"], i,
                 final_gain=p["norm_final"] if i == DEPTH - 1 else None)
    return x


def kernel(x_prompt, x_sample, norm_ffn1, ffn1_w_in, ffn1_w_out, norm_mix, norm_ffn2, ffn2_w_in, ffn2_w_out,
           ret_w_in, ret_w_out, ret_decay_f, ret_decay_b, ret_gn_f, ret_gn_b, na_w_in, na_w_out, na_rpb,
           norm_final):
    n_ret = ret_w_in.shape[0]
    n_na = na_w_in.shape[0]
    bf = lambda w: w.astype(BF16)
    p = dict(
        norm_ffn1=norm_ffn1, norm_mix=norm_mix, norm_ffn2=norm_ffn2, norm_final=norm_final,
        ffn1_w_in=bf(ffn1_w_in), ffn1_w_out=bf(ffn1_w_out), ffn2_w_in=bf(ffn2_w_in), ffn2_w_out=bf(ffn2_w_out),
        ret_w_in=bf(ret_w_in), ret_w_out=bf(ret_w_out), ret_gn_f=ret_gn_f, ret_gn_b=ret_gn_b,
        ret_tables=[_retention_tables(ret_decay_f[j], ret_decay_b[j]) for j in range(n_ret)],
        rotary=_rotary_tables(),
        na_w_in=bf(na_w_in), na_w_out=bf(na_w_out),
        na_bias=[_na_bias_tables(na_rpb[j]) for j in range(n_na)],
    )
    outs = []
    for x in (x_prompt, x_sample):
        y = _trunk(x.reshape(-1, D_MODEL), p)
        outs.append(y.reshape(x.shape))
    return tuple(outs)
```

```python
import functools

import numpy as np
import jax
import jax.numpy as jnp
from jax import lax
from jax.experimental import pallas as pl
from jax.experimental.pallas import tpu as pltpu

D_MODEL = 2048
SEQ = 4096
DEPTH = 2
GRID_W = 64
D_FF = 5632
RET_HEADS = 8
RET_DK = D_MODEL // RET_HEADS
RET_DV = 2 * RET_DK
RET_VDIM = RET_HEADS * RET_DV
RET_IN = 2 * D_MODEL + 3 * RET_VDIM
ROPE_BASE = 10000.0
NA_HEADS = 16
NA_HEAD_DIM = D_MODEL // NA_HEADS
NA_KH = 8
NA_KW = 16
RMS_EPS = 1e-6
MASK_VALUE = -1e30
LOG2_E = 1.4426950408889634

F32 = jnp.float32
BF16 = jnp.bfloat16

V7X_VMEM_BYTES = 64 * 1024 * 1024
VMEM_LIMIT_BYTES = V7X_VMEM_BYTES - 8 * 1024 * 1024

FFN_TM, FFN_TF = 1024, 512
PROJ_TM, PROJ_TN = 1024, 2048
OUT_TM = 512
SCAN_TC = 2048
SCAN_CHUNK = 256
GRID_ROWS = SEQ // GRID_W


def _params(semantics):
    return pltpu.CompilerParams(dimension_semantics=semantics, vmem_limit_bytes=VMEM_LIMIT_BYTES)


def _rmsnorm(xf, g):
    ms = jnp.mean(xf * xf, axis=-1, keepdims=True)
    return xf * lax.rsqrt(ms + RMS_EPS) * g


def _silu(g):
    return g / (1.0 + jnp.exp(-g))


def _ffn_kernel(x_ref, g_ref, wg_ref, wu_ref, wo_ref, *rest, final_norm):
    if final_norm:
        gfin_ref, o_ref, h_ref = rest
    else:
        o_ref, h_ref = rest
    j = pl.program_id(1)

    @pl.when(j == 0)
    def _():
        x = x_ref[...]
        h_ref[...] = _rmsnorm(x, g_ref[...]).astype(BF16)
        o_ref[...] = x

    h = h_ref[...]
    g = jnp.dot(h, wg_ref[...], preferred_element_type=F32)
    u = jnp.dot(h, wu_ref[...], preferred_element_type=F32)
    a = (_silu(g) * u * 0.5).astype(BF16)
    o_ref[...] += jnp.dot(a, wo_ref[...], preferred_element_type=F32)

    if final_norm:
        @pl.when(j == pl.num_programs(1) - 1)
        def _():
            o_ref[...] = _rmsnorm(o_ref[...], gfin_ref[...])


def _ffn(x, gain, w_in, w_out, layer, final_gain=None):
    t = x.shape[0]
    nj = D_FF // FFN_TF
    in_specs = [
        pl.BlockSpec((FFN_TM, D_MODEL), lambda i, j: (i, 0)),
        pl.BlockSpec((1, D_MODEL), lambda i, j: (0, 0)),
        pl.BlockSpec((None, D_MODEL, FFN_TF), lambda i, j: (layer, 0, j)),
        pl.BlockSpec((None, D_MODEL, FFN_TF), lambda i, j: (layer, 0, nj + j)),
        pl.BlockSpec((None, FFN_TF, D_MODEL), lambda i, j: (layer, j, 0)),
    ]
    args = [x, gain.reshape(1, D_MODEL), w_in, w_in, w_out]
    if final_gain is not None:
        in_specs.append(pl.BlockSpec((1, D_MODEL), lambda i, j: (0, 0)))
        args.append(final_gain.reshape(1, D_MODEL))
    return pl.pallas_call(
        functools.partial(_ffn_kernel, final_norm=final_gain is not None),
        grid=(t // FFN_TM, nj),
        in_specs=in_specs,
        out_specs=pl.BlockSpec((FFN_TM, D_MODEL), lambda i, j: (i, 0)),
        out_shape=jax.ShapeDtypeStruct((t, D_MODEL), F32),
        scratch_shapes=[pltpu.VMEM((FFN_TM, D_MODEL), BF16)],
        compiler_params=_params(("parallel", "arbitrary")),
        name="swiglu_ffn",
    )(*args)


def _proj_kernel(*refs, epilogue, h_mode):
    refs = list(refs)
    if h_mode == "input":
        h_ref = refs.pop(0)
    else:
        x_ref, g_ref = refs.pop(0), refs.pop(0)
    w_ref = refs.pop(0)
    if epilogue == "rotary":
        cos_ref, sin_ref = refs.pop(0), refs.pop(0)
    elif epilogue == "gate":
        gn_ref = refs.pop(0)
    o_ref = refs.pop(0)
    if h_mode != "input":
        h_ref = refs.pop(0)
    n = pl.program_id(1)

    if h_mode != "input":
        @pl.when(n == 0)
        def _():
            h_ref[...] = _rmsnorm(x_ref[...], g_ref[...]).astype(BF16)

    y = jnp.dot(h_ref[...], w_ref[...], preferred_element_type=F32)
    if epilogue == "plain":
        o_ref[...] = y.astype(BF16)
        return
    if epilogue == "gate":
        o_ref[...] = (_silu(y) * gn_ref[...]).astype(BF16)
        return
    cos = cos_ref[...]
    sin = sin_ref[...]
    scale = jnp.where(n < pl.num_programs(1) // 2, 1.0, RET_DK ** -0.5).astype(F32)
    half = RET_DK // 2
    for hh in range(PROJ_TN // RET_DK):
        x1 = y[:, hh * RET_DK: hh * RET_DK + half]
        x2 = y[:, hh * RET_DK + half: (hh + 1) * RET_DK]
        o_ref[:, hh * RET_DK: hh * RET_DK + half] = ((x1 * cos - x2 * sin) * scale).astype(BF16)
        o_ref[:, hh * RET_DK + half: (hh + 1) * RET_DK] = ((x1 * sin + x2 * cos) * scale).astype(BF16)


def _proj(x, gain, w, layer, cols, rotary=None, gate_gain=None, emit_h=False):
    t = x.shape[0]
    n_out = cols[1] - cols[0]
    col0 = cols[0] // PROJ_TN
    row_spec = pl.BlockSpec((PROJ_TM, D_MODEL), lambda i, n: (i, 0))
    h_mode = "input" if gain is None else ("emit" if emit_h else "scratch")
    in_specs = [row_spec]
    args = [x]
    if gain is not None:
        in_specs.append(pl.BlockSpec((1, D_MODEL), lambda i, n: (0, 0)))
        args.append(gain.reshape(1, D_MODEL))
    in_specs.append(pl.BlockSpec((None, D_MODEL, PROJ_TN), lambda i, n: (layer, 0, col0 + n)))
    args.append(w)
    epilogue = "plain"
    if rotary is not None:
        epilogue = "rotary"
        seq_tiles = SEQ // PROJ_TM
        tab_spec = pl.BlockSpec((PROJ_TM, RET_DK // 2), lambda i, n: (i % seq_tiles, 0))
        in_specs += [tab_spec, tab_spec]
        args += list(rotary)
    elif gate_gain is not None:
        epilogue = "gate"
        in_specs.append(pl.BlockSpec((1, PROJ_TN), lambda i, n: (0, n)))
        args.append(gate_gain.reshape(1, n_out).astype(F32))
    out_specs = pl.BlockSpec((PROJ_TM, PROJ_TN), lambda i, n: (i, n))
    out_shape = jax.ShapeDtypeStruct((t, n_out), BF16)
    if h_mode == "emit":
        out_specs = [out_specs, row_spec]
        out_shape = [out_shape, jax.ShapeDtypeStruct((t, D_MODEL), BF16)]
    return pl.pallas_call(
        functools.partial(_proj_kernel, epilogue=epilogue, h_mode=h_mode),
        grid=(t // PROJ_TM, n_out // PROJ_TN),
        in_specs=in_specs,
        out_specs=out_specs,
        out_shape=out_shape,
        scratch_shapes=[pltpu.VMEM((PROJ_TM, D_MODEL), BF16)] if h_mode == "scratch" else [],
        compiler_params=_params(("parallel", "arbitrary")),
        name="proj_" + epilogue + "_h" + h_mode,
    )(*args)


def _out_kernel(a_ref, w_ref, x_ref, o_ref):
    o_ref[...] = x_ref[...] + jnp.dot(a_ref[...], w_ref[...], preferred_element_type=F32)


def _out_proj(a, w, layer, x):
    t, k = a.shape
    return pl.pallas_call(
        _out_kernel,
        grid=(t // OUT_TM,),
        in_specs=[
            pl.BlockSpec((OUT_TM, k), lambda i: (i, 0)),
            pl.BlockSpec((None, k, D_MODEL), lambda i: (layer, 0, 0), pipeline_mode=pl.Buffered(1)),
            pl.BlockSpec((OUT_TM, D_MODEL), lambda i: (i, 0)),
        ],
        out_specs=pl.BlockSpec((OUT_TM, D_MODEL), lambda i: (i, 0)),
        out_shape=jax.ShapeDtypeStruct((t, D_MODEL), F32),
        compiler_params=_params(("parallel",)),
        name="out_proj_residual",
    )(a, w, x)


SCAN_NBLK = SEQ // SCAN_TC
SCAN_CHUNKS = SCAN_TC // SCAN_CHUNK


def _scan_kernel(dec_ref, q_ref, k_ref, v_ref, gf_ref, gb_ref, tab_ref, o_ref, r_ref, ob_ref):
    h = pl.program_id(1)
    t = pl.program_id(2)
    c_len = SCAN_CHUNK

    @pl.when((t == 0) | (t == SCAN_NBLK))
    def _():
        r_ref[...] = jnp.zeros_like(r_ref)

    def run(backward):
        inner = tab_ref[0]
        cross = jnp.concatenate([tab_ref[1]] * (RET_DV // c_len), axis=1)
        into = jnp.concatenate([tab_ref[2]] * (RET_DK // c_len), axis=1)
        dec = dec_ref[1 if backward else 0, h]
        gate_ref = gb_ref if backward else gf_ref
        blk = (SCAN_NBLK - 1 - t) if backward else (t - SCAN_NBLK)
        order = range(SCAN_CHUNKS - 1, -1, -1) if backward else range(SCAN_CHUNKS)
        for c in order:
            rows = pl.ds(c * c_len, c_len)
            seq_rows = pl.ds(pl.multiple_of(blk * SCAN_TC + c * c_len, c_len), c_len)
            q = q_ref[seq_rows, :]
            k = k_ref[seq_rows, :]
            v = v_ref[seq_rows, :]
            s = lax.dot_general(q, k, (((1,), (1,)), ((), ())), preferred_element_type=F32) * inner
            kd = (k.astype(F32) * into).astype(BF16)
            r = r_ref[...]
            sv = jnp.dot(jnp.concatenate([s.astype(BF16), kd.T], axis=0), v, preferred_element_type=F32)
            o = sv[:c_len] + jnp.dot(q, r.astype(BF16), preferred_element_type=F32) * cross
            r_ref[...] = r * dec + sv[c_len:]
            on = o * lax.rsqrt(jnp.mean(o * o, axis=-1, keepdims=True) + RMS_EPS)
            m = gate_ref[rows, :].astype(F32) * on
            if backward:
                ob_ref[seq_rows, :] = m
            else:
                o_ref[rows, :] = (m + ob_ref[seq_rows, :]).astype(BF16)

    @pl.when(t < SCAN_NBLK)
    def _():
        o_ref[...] = jnp.zeros_like(o_ref)
        run(True)

    @pl.when(t >= SCAN_NBLK)
    def _():
        run(False)


def _retention_scan(qk, v, gates, tabs, dec):
    t = qk.shape[0]
    batch = t // SEQ
    nb = SCAN_NBLK

    def bwd_blk(s):
        return jnp.where(s < nb, nb - 1 - s, 0)

    def fwd_blk(s):
        return jnp.where(s < nb, 0, s - nb)

    nh = RET_HEADS
    in_specs = [
        pl.BlockSpec(memory_space=pltpu.SMEM),
        pl.BlockSpec((SEQ, RET_DK), lambda b, h, s: (b, h)),
        pl.BlockSpec((SEQ, RET_DK), lambda b, h, s: (b, nh + h)),
        pl.BlockSpec((SEQ, RET_DV), lambda b, h, s: (b, h)),
        pl.BlockSpec((SCAN_TC, RET_DV), lambda b, h, s: (b * nb + fwd_blk(s), h)),
        pl.BlockSpec((SCAN_TC, RET_DV), lambda b, h, s: (b * nb + bwd_blk(s), nh + h)),
        pl.BlockSpec((None, None, 3, SCAN_CHUNK, SCAN_CHUNK),
                     lambda b, h, s: (jnp.where(s < nb, 1, 0), h, 0, 0, 0)),
    ]
    return pl.pallas_call(
        _scan_kernel,
        grid=(batch, RET_HEADS, 2 * nb),
        in_specs=in_specs,
        out_specs=pl.BlockSpec((SCAN_TC, RET_DV), lambda b, h, s: (b * nb + fwd_blk(s), h)),
        out_shape=jax.ShapeDtypeStruct((t, RET_VDIM), BF16),
        scratch_shapes=[pltpu.VMEM((RET_DK, RET_DV), F32), pltpu.VMEM((SEQ, RET_DV), F32)],
        compiler_params=_params(("arbitrary", "arbitrary", "arbitrary")),
        name="retention_scan",
    )(dec, qk, qk, v, gates, gates, tabs)


def _retention_tables(decay_f, decay_b):
    c = SCAN_CHUNK
    pos = jnp.arange(c, dtype=F32)
    ones = jnp.ones((1, 1, c), F32)

    def tables(log_gamma, backward):
        lg = log_gamma[:, None, None]
        diff = pos[:, None] - pos[None, :]
        if backward:
            diff = -diff
        inner = jnp.where(diff[None] >= 0, jnp.exp(lg * jnp.maximum(diff, 0.0)[None]), 0.0)
        cross_pos = (c - pos) if backward else (pos + 1.0)
        into_pos = pos if backward else (c - 1.0 - pos)
        cross = jnp.exp(log_gamma[:, None] * cross_pos[None])[:, :, None] * ones
        into = jnp.exp(log_gamma[:, None] * into_pos[None])[:, :, None] * ones
        return jnp.stack([inner, cross, into], axis=1), jnp.exp(log_gamma * c)

    lg_f = jnp.log1p(-jnp.exp(decay_f.astype(F32)))
    lg_b = jnp.log1p(-jnp.exp(decay_b.astype(F32)))
    tab_f, dec_f = tables(lg_f, False)
    tab_b, dec_b = tables(lg_b, True)
    return jnp.stack([tab_f, tab_b]), jnp.stack([dec_f, dec_b])


def _rotary_tables():
    d = RET_DK
    inv = ROPE_BASE ** (-jnp.arange(0, d, 2, dtype=F32) / d)
    ang = jnp.arange(SEQ, dtype=F32)[:, None] * inv[None, :]
    return jnp.cos(ang), jnp.sin(ang)


NA_QROWS = 4
NA_GROUPS = GRID_ROWS // NA_QROWS
NA_SLAB_ROWS = NA_QROWS + NA_KH
NA_QTOK = NA_QROWS * GRID_W
NA_SLAB = NA_SLAB_ROWS * GRID_W


NA_GEOMETRIES = (
    lambda rq, rk: (rk < NA_KH, rk - rq + NA_KH - 1),
    lambda rq, rk: (rq <= rk < rq + NA_KH, rk - rq + NA_KH // 2 - 1),
    lambda rq, rk: (rk >= NA_QROWS, rk - rq - 1),
)


def _na_kernel(q_ref, k_ref, v_ref, toe_ref, o_ref, s_ref, bias_ref):
    scale = NA_HEAD_DIM ** -0.5 * LOG2_E

    @pl.when(pl.program_id(1) == 0)
    def _():
        first_row = lax.broadcasted_iota(jnp.int32, (GRID_W, 2 * GRID_W), 1) < GRID_W
        masked = jnp.full((GRID_W, 2 * GRID_W), MASK_VALUE, F32)
        for kind, geometry in enumerate(NA_GEOMETRIES):
            for rq in range(NA_QROWS):
                for rk in range(0, NA_SLAB_ROWS, 2):
                    (ok0, dr0), (ok1, dr1) = geometry(rq, rk), geometry(rq, rk + 1)
                    tile = jnp.where(first_row, toe_ref[dr0] if ok0 else masked,
                                     toe_ref[dr1] if ok1 else masked) if (ok0 or ok1) else masked
                    bias_ref[kind, rq * GRID_W:(rq + 1) * GRID_W, rk * GRID_W:(rk + 2) * GRID_W] = tile

    def windows(g):
        s0 = jnp.clip(NA_QROWS * g - NA_KH // 2, 0, GRID_ROWS - NA_SLAB_ROWS)
        rows = pl.ds(pl.multiple_of(g * NA_QTOK, NA_QTOK), NA_QTOK)
        slab = pl.ds(pl.multiple_of(s0 * GRID_W, NA_QTOK), NA_SLAB)
        return rows, slab

    def scores(g, slot):
        rows, slab = windows(g)
        kind = jnp.where(g == 0, 0, jnp.where(g == NA_GROUPS - 1, 2, 1))
        s = lax.dot_general(q_ref[rows, :], k_ref[slab, :], (((1,), (1,)), ((), ())),
                            preferred_element_type=F32)
        s_ref[slot] = s * scale + bias_ref[kind]

    def attend(g, slot):
        rows, slab = windows(g)
        s = s_ref[slot]
        e = jnp.exp2(s - jnp.max(s, axis=-1, keepdims=True))
        o = jnp.dot(e.astype(BF16), v_ref[slab, :], preferred_element_type=F32)
        o_ref[rows, :] = (o * (1.0 / jnp.sum(e, axis=-1, keepdims=True))).astype(BF16)

    scores(0, 0)

    def pair(i, carry):
        g = 2 * i
        scores(g + 1, 1)
        attend(g, 0)
        scores(g + 2, 0)
        attend(g + 1, 1)
        return carry

    lax.fori_loop(0, NA_GROUPS // 2 - 1, pair, 0, unroll=True)
    scores(NA_GROUPS - 1, 1)
    attend(NA_GROUPS - 2, 0)
    attend(NA_GROUPS - 1, 1)


def _na_bias_tables(rpb):
    pad = GRID_W - NA_KW
    padded = jnp.pad(rpb.astype(F32), ((0, 0), (0, 0), (pad, pad)))
    toe = jnp.stack([padded[:, :, GRID_W - 1 - c: 2 * GRID_W - 1 - c] for c in range(GRID_W)], axis=2)
    cq = np.arange(GRID_W)
    ck = np.arange(GRID_W)
    qwin = np.clip(cq - NA_KW // 2, 0, GRID_W - NA_KW)
    col_ok = (ck[None, :] >= qwin[:, None]) & (ck[None, :] < qwin[:, None] + NA_KW)
    toe = jnp.where(col_ok[None, None], toe * LOG2_E, MASK_VALUE)
    return jnp.concatenate([toe, toe], axis=-1)


def _neighbourhood_attention(qkv, bias):
    t = qkv.shape[0]
    batch = t // SEQ
    hd = NA_HEAD_DIM
    return pl.pallas_call(
        _na_kernel,
        grid=(NA_HEADS, batch),
        in_specs=[
            pl.BlockSpec((SEQ, hd), lambda h, b: (b, h)),
            pl.BlockSpec((SEQ, hd), lambda h, b: (b, NA_HEADS + h)),
            pl.BlockSpec((SEQ, hd), lambda h, b: (b, 2 * NA_HEADS + h)),
            pl.BlockSpec((None, 2 * NA_KH - 1, GRID_W, 2 * GRID_W), lambda h, b: (h, 0, 0, 0)),
        ],
        out_specs=pl.BlockSpec((SEQ, hd), lambda h, b: (b, h)),
        out_shape=jax.ShapeDtypeStruct((t, D_MODEL), BF16),
        scratch_shapes=[pltpu.VMEM((2, NA_QTOK, NA_SLAB), F32), pltpu.VMEM((3, NA_QTOK, NA_SLAB), F32)],
        compiler_params=_params(("arbitrary", "arbitrary")),
        name="neighbourhood_attention",
    )(qkv, qkv, qkv, bias)


def _trunk(x, p):
    for i in range(DEPTH):
        x = _ffn(x, p["norm_ffn1"][i], p["ffn1_w_in"], p["ffn1_w_out"], i)
        j = i // 2
        if i % 2 == 0:
            qk, h = _proj(x, p["norm_mix"][i], p["ret_w_in"], j, (0, 2 * D_MODEL), rotary=p["rotary"],
                          emit_h=True)
            v = _proj(h, None, p["ret_w_in"], j, (2 * D_MODEL, 2 * D_MODEL + RET_VDIM))
            gates = _proj(h, None, p["ret_w_in"], j, (2 * D_MODEL + RET_VDIM, RET_IN),
                          gate_gain=jnp.concatenate([p["ret_gn_f"][j], p["ret_gn_b"][j]]))
            tabs, dec = p["ret_tables"][j]
            mixed = _retention_scan(qk, v, gates, tabs, dec)
            x = _out_proj(mixed, p["ret_w_out"], j, x)
        else:
            qkv = _proj(x, p["norm_mix"][i], p["na_w_in"], j, (0, 3 * D_MODEL))
            att = _neighbourhood_attention(qkv, p["na_bias"][j])
            x = _out_proj(att, p["na_w_out"], j, x)
        x = _ffn(x, p["norm_ffn2"][i], p["ffn2_w_in"], p["ffn2_w_out"], i,
                 final_gain=p["norm_final"] if i == DEPTH - 1 else None)
    return x


def kernel(x_prompt, x_sample, norm_ffn1, ffn1_w_in, ffn1_w_out, norm_mix, norm_ffn2, ffn2_w_in, ffn2_w_out,
           ret_w_in, ret_w_out, ret_decay_f, ret_decay_b, ret_gn_f, ret_gn_b, na_w_in, na_w_out, na_rpb,
           norm_final):
    n_ret = ret_w_in.shape[0]
    n_na = na_w_in.shape[0]
    bf = lambda w: w.astype(BF16)
    p = dict(
        norm_ffn1=norm_ffn1, norm_mix=norm_mix, norm_ffn2=norm_ffn2, norm_final=norm_final,
        ffn1_w_in=bf(ffn1_w_in), ffn1_w_out=bf(ffn1_w_out), ffn2_w_in=bf(ffn2_w_in), ffn2_w_out=bf(ffn2_w_out),
        ret_w_in=bf(ret_w_in), ret_w_out=bf(ret_w_out), ret_gn_f=ret_gn_f, ret_gn_b=ret_gn_b,
        ret_tables=[_retention_tables(ret_decay_f[j], ret_decay_b[j]) for j in range(n_ret)],
        rotary=_rotary_tables(),
        na_w_in=bf(na_w_in), na_w_out=bf(na_w_out),
        na_bias=[_na_bias_tables(na_rpb[j]) for j in range(n_na)],
    )
    outs = []
    for x in (x_prompt, x_sample):
        y = _trunk(x.reshape(-1, D_MODEL), p)
        outs.append(y.reshape(x.shape))
    return tuple(outs)
```

```python
import functools

import numpy as np
import jax
import jax.numpy as jnp
from jax import lax
from jax.experimental import pallas as pl
from jax.experimental.pallas import tpu as pltpu

D_MODEL = 2048
SEQ = 4096
DEPTH = 2
GRID_W = 64
D_FF = 5632
RET_HEADS = 8
RET_DK = D_MODEL // RET_HEADS
RET_DV = 2 * RET_DK
RET_VDIM = RET_HEADS * RET_DV
RET_IN = 2 * D_MODEL + 3 * RET_VDIM
ROPE_BASE = 10000.0
NA_HEADS = 16
NA_HEAD_DIM = D_MODEL // NA_HEADS
NA_KH = 8
NA_KW = 16
RMS_EPS = 1e-6
MASK_VALUE = -1e30
LOG2_E = 1.4426950408889634

F32 = jnp.float32
BF16 = jnp.bfloat16

V7X_VMEM_BYTES = 64 * 1024 * 1024
VMEM_LIMIT_BYTES = V7X_VMEM_BYTES - 8 * 1024 * 1024

FFN_TM, FFN_TF = 1024, 512
PROJ_TM, PROJ_TN = 1024, 2048
OUT_TM = 512
SCAN_TC = 2048
SCAN_CHUNK = 256
GRID_ROWS = SEQ // GRID_W


def _params(semantics):
    return pltpu.CompilerParams(dimension_semantics=semantics, vmem_limit_bytes=VMEM_LIMIT_BYTES)


def _rmsnorm(xf, g):
    ms = jnp.mean(xf * xf, axis=-1, keepdims=True)
    return xf * lax.rsqrt(ms + RMS_EPS) * g


def _silu(g):
    return g / (1.0 + jnp.exp(-g))


def _ffn_kernel(x_ref, g_ref, wg_ref, wu_ref, wo_ref, *rest, final_norm):
    if final_norm:
        gfin_ref, o_ref, h_ref = rest
    else:
        o_ref, h_ref = rest
    j = pl.program_id(1)

    @pl.when(j == 0)
    def _():
        x = x_ref[...]
        h_ref[...] = _rmsnorm(x, g_ref[...]).astype(BF16)
        o_ref[...] = x

    h = h_ref[...]
    g = jnp.dot(h, wg_ref[...], preferred_element_type=F32)
    u = jnp.dot(h, wu_ref[...], preferred_element_type=F32)
    a = (_silu(g) * u * 0.5).astype(BF16)
    o_ref[...] += jnp.dot(a, wo_ref[...], preferred_element_type=F32)

    if final_norm:
        @pl.when(j == pl.num_programs(1) - 1)
        def _():
            o_ref[...] = _rmsnorm(o_ref[...], gfin_ref[...])


def _ffn(x, gain, w_in, layer_in, w_out, layer_out, final_gain=None):
    t = x.shape[0]
    nj = D_FF // FFN_TF
    in_specs = [
        pl.BlockSpec((FFN_TM, D_MODEL), lambda i, j: (i, 0)),
        pl.BlockSpec((1, D_MODEL), lambda i, j: (0, 0)),
        pl.BlockSpec((None, D_MODEL, FFN_TF), lambda i, j: (layer_in, 0, j)),
        pl.BlockSpec((None, D_MODEL, FFN_TF), lambda i, j: (layer_in, 0, nj + j)),
        pl.BlockSpec((None, FFN_TF, D_MODEL), lambda i, j: (layer_out, j, 0)),
    ]
    args = [x, gain.reshape(1, D_MODEL), w_in, w_in, w_out]
    if final_gain is not None:
        in_specs.append(pl.BlockSpec((1, D_MODEL), lambda i, j: (0, 0)))
        args.append(final_gain.reshape(1, D_MODEL))
    return pl.pallas_call(
        functools.partial(_ffn_kernel, final_norm=final_gain is not None),
        grid=(t // FFN_TM, nj),
        in_specs=in_specs,
        out_specs=pl.BlockSpec((FFN_TM, D_MODEL), lambda i, j: (i, 0)),
        out_shape=jax.ShapeDtypeStruct((t, D_MODEL), F32),
        scratch_shapes=[pltpu.VMEM((FFN_TM, D_MODEL), BF16)],
        compiler_params=_params(("parallel", "arbitrary")),
        name="swiglu_ffn",
    )(*args)


SIDE_CHUNKS = 32


def _side_cast_specs(jobs, step_of):
    in_specs, out_specs, out_shapes, args = [], [], [], []
    for w, layer in jobs:
        _, r, c = w.shape
        rows = r // SIDE_CHUNKS
        assert rows * SIDE_CHUNKS == r and rows % 16 == 0

        def chunk(*g):
            return jnp.minimum(step_of(*g), SIDE_CHUNKS - 1)

        in_specs.append(pl.BlockSpec((None, rows, c), lambda *g, layer=layer: (layer, chunk(*g), 0)))
        out_specs.append(pl.BlockSpec((None, rows, c), lambda *g: (0, chunk(*g), 0)))
        out_shapes.append(jax.ShapeDtypeStruct((1, r, c), BF16))
        args.append(w)
    return in_specs, out_specs, out_shapes, args


def _side_cast(srcs, dsts):
    for src, dst in zip(srcs, dsts):
        dst[...] = src[...].astype(BF16)


def _proj_kernel(*refs, epilogue, h_mode, n_side):
    refs = list(refs)
    if h_mode == "input":
        h_ref = refs.pop(0)
    else:
        x_ref, g_ref = refs.pop(0), refs.pop(0)
    w_ref = refs.pop(0)
    if epilogue == "rotary":
        cos_ref, sin_ref = refs.pop(0), refs.pop(0)
    elif epilogue == "gate":
        gn_ref = refs.pop(0)
    side_srcs = [refs.pop(0) for _ in range(n_side)]
    o_ref = refs.pop(0)
    if h_mode == "emit":
        h_ref = refs.pop(0)
    side_dsts = [refs.pop(0) for _ in range(n_side)]
    if h_mode == "scratch":
        h_ref = refs.pop(0)
    n = pl.program_id(1)
    _side_cast(side_srcs, side_dsts)

    if h_mode != "input":
        @pl.when(n == 0)
        def _():
            h_ref[...] = _rmsnorm(x_ref[...], g_ref[...]).astype(BF16)

    y = jnp.dot(h_ref[...], w_ref[...], preferred_element_type=F32)
    if epilogue == "plain":
        o_ref[...] = y.astype(BF16)
        return
    if epilogue == "gate":
        o_ref[...] = (_silu(y) * gn_ref[...]).astype(BF16)
        return
    cos = cos_ref[...]
    sin = sin_ref[...]
    scale = jnp.where(n < pl.num_programs(1) // 2, 1.0, RET_DK ** -0.5).astype(F32)
    half = RET_DK // 2
    for hh in range(PROJ_TN // RET_DK):
        x1 = y[:, hh * RET_DK: hh * RET_DK + half]
        x2 = y[:, hh * RET_DK + half: (hh + 1) * RET_DK]
        o_ref[:, hh * RET_DK: hh * RET_DK + half] = ((x1 * cos - x2 * sin) * scale).astype(BF16)
        o_ref[:, hh * RET_DK + half: (hh + 1) * RET_DK] = ((x1 * sin + x2 * cos) * scale).astype(BF16)


def _proj(x, gain, w, layer, cols, rotary=None, gate_gain=None, emit_h=False, side=()):
    t = x.shape[0]
    n_out = cols[1] - cols[0]
    col0 = cols[0] // PROJ_TN
    grid = (t // PROJ_TM, n_out // PROJ_TN)
    row_spec = pl.BlockSpec((PROJ_TM, D_MODEL), lambda i, n: (i, 0))
    h_mode = "input" if gain is None else ("emit" if emit_h else "scratch")
    in_specs = [row_spec]
    args = [x]
    if gain is not None:
        in_specs.append(pl.BlockSpec((1, D_MODEL), lambda i, n: (0, 0)))
        args.append(gain.reshape(1, D_MODEL))
    in_specs.append(pl.BlockSpec((None, D_MODEL, PROJ_TN), lambda i, n: (layer, 0, col0 + n)))
    args.append(w)
    epilogue = "plain"
    if rotary is not None:
        epilogue = "rotary"
        seq_tiles = SEQ // PROJ_TM
        tab_spec = pl.BlockSpec((PROJ_TM, RET_DK // 2), lambda i, n: (i % seq_tiles, 0))
        in_specs += [tab_spec, tab_spec]
        args += list(rotary)
    elif gate_gain is not None:
        epilogue = "gate"
        in_specs.append(pl.BlockSpec((1, PROJ_TN), lambda i, n: (0, n)))
        args.append(gate_gain.reshape(1, n_out).astype(F32))
    assert not side or grid[0] * grid[1] >= SIDE_CHUNKS
    s_in, s_out, s_shapes, s_args = _side_cast_specs(side, lambda i, n: i * grid[1] + n)
    out_specs = [pl.BlockSpec((PROJ_TM, PROJ_TN), lambda i, n: (i, n))]
    out_shape = [jax.ShapeDtypeStruct((t, n_out), BF16)]
    if h_mode == "emit":
        out_specs.append(row_spec)
        out_shape.append(jax.ShapeDtypeStruct((t, D_MODEL), BF16))
    outs = pl.pallas_call(
        functools.partial(_proj_kernel, epilogue=epilogue, h_mode=h_mode, n_side=len(side)),
        grid=grid,
        in_specs=in_specs + s_in,
        out_specs=out_specs + s_out,
        out_shape=out_shape + s_shapes,
        scratch_shapes=[pltpu.VMEM((PROJ_TM, D_MODEL), BF16)] if h_mode == "scratch" else [],
        compiler_params=_params(("parallel", "arbitrary")),
        name="proj_" + epilogue + "_h" + h_mode,
    )(*args, *s_args)
    n_main = len(out_specs)
    return outs[0], (outs[1] if h_mode == "emit" else None), list(outs[n_main:])


def _out_kernel(a_ref, w_ref, x_ref, *rest, n_side):
    side_srcs, o_ref, side_dsts = rest[:n_side], rest[n_side], rest[n_side + 1:]
    _side_cast(side_srcs, side_dsts)
    o_ref[...] = x_ref[...] + jnp.dot(a_ref[...], w_ref[...], preferred_element_type=F32)


def _out_proj(a, w, layer, x, side=()):
    t, k = a.shape
    steps = t // OUT_TM
    assert not side or steps >= SIDE_CHUNKS
    s_in, s_out, s_shapes, s_args = _side_cast_specs(side, lambda i: i)
    outs = pl.pallas_call(
        functools.partial(_out_kernel, n_side=len(side)),
        grid=(steps,),
        in_specs=[
            pl.BlockSpec((OUT_TM, k), lambda i: (i, 0)),
            pl.BlockSpec((None, k, D_MODEL), lambda i: (layer, 0, 0), pipeline_mode=pl.Buffered(1)),
            pl.BlockSpec((OUT_TM, D_MODEL), lambda i: (i, 0)),
        ] + s_in,
        out_specs=[pl.BlockSpec((OUT_TM, D_MODEL), lambda i: (i, 0))] + s_out,
        out_shape=[jax.ShapeDtypeStruct((t, D_MODEL), F32)] + s_shapes,
        compiler_params=_params(("parallel",)),
        name="out_proj_residual",
    )(a, w, x, *s_args)
    return outs[0], list(outs[1:])


SCAN_NBLK = SEQ // SCAN_TC
SCAN_CHUNKS = SCAN_TC // SCAN_CHUNK


def _scan_kernel(dec_ref, q_ref, k_ref, v_ref, gf_ref, gb_ref, tab_ref, o_ref, r_ref, ob_ref):
    h = pl.program_id(1)
    t = pl.program_id(2)
    c_len = SCAN_CHUNK

    @pl.when((t == 0) | (t == SCAN_NBLK))
    def _():
        r_ref[...] = jnp.zeros_like(r_ref)

    def run(backward):
        inner = tab_ref[0]
        cross = jnp.concatenate([tab_ref[1]] * (RET_DV // c_len), axis=1)
        into = jnp.concatenate([tab_ref[2]] * (RET_DK // c_len), axis=1)
        dec = dec_ref[1 if backward else 0, h]
        gate_ref = gb_ref if backward else gf_ref
        blk = (SCAN_NBLK - 1 - t) if backward else (t - SCAN_NBLK)
        order = range(SCAN_CHUNKS - 1, -1, -1) if backward else range(SCAN_CHUNKS)
        for c in order:
            rows = pl.ds(c * c_len, c_len)
            seq_rows = pl.ds(pl.multiple_of(blk * SCAN_TC + c * c_len, c_len), c_len)
            q = q_ref[rows, :]
            k = k_ref[rows, :]
            v = v_ref[rows, :]
            s = lax.dot_general(q, k, (((1,), (1,)), ((), ())), preferred_element_type=F32) * inner
            kd = (k.astype(F32) * into).astype(BF16)
            r = r_ref[...]
            sv = jnp.dot(jnp.concatenate([s.astype(BF16), kd.T], axis=0), v, preferred_element_type=F32)
            o = sv[:c_len] + jnp.dot(q, r.astype(BF16), preferred_element_type=F32) * cross
            r_ref[...] = r * dec + sv[c_len:]
            on = o * lax.rsqrt(jnp.mean(o * o, axis=-1, keepdims=True) + RMS_EPS)
            m = gate_ref[rows, :].astype(F32) * on
            if backward:
                ob_ref[seq_rows, :] = m
            else:
                o_ref[rows, :] = (m + ob_ref[seq_rows, :]).astype(BF16)

    @pl.when(t < SCAN_NBLK)
    def _():
        o_ref[...] = jnp.zeros_like(o_ref)
        run(True)

    @pl.when(t >= SCAN_NBLK)
    def _():
        run(False)


def _retention_scan(qk, v, gates, tabs, dec):
    t = qk.shape[0]
    batch = t // SEQ
    nb = SCAN_NBLK

    def bwd_blk(s):
        return jnp.where(s < nb, nb - 1 - s, 0)

    def fwd_blk(s):
        return jnp.where(s < nb, 0, s - nb)

    def cur_blk(s):
        return jnp.where(s < nb, nb - 1 - s, s - nb)

    nh = RET_HEADS
    in_specs = [
        pl.BlockSpec(memory_space=pltpu.SMEM),
        pl.BlockSpec((SCAN_TC, RET_DK), lambda b, h, s: (b * nb + cur_blk(s), h)),
        pl.BlockSpec((SCAN_TC, RET_DK), lambda b, h, s: (b * nb + cur_blk(s), nh + h)),
        pl.BlockSpec((SCAN_TC, RET_DV), lambda b, h, s: (b * nb + cur_blk(s), h)),
        pl.BlockSpec((SCAN_TC, RET_DV), lambda b, h, s: (b * nb + fwd_blk(s), h)),
        pl.BlockSpec((SCAN_TC, RET_DV), lambda b, h, s: (b * nb + bwd_blk(s), nh + h)),
        pl.BlockSpec((None, None, 3, SCAN_CHUNK, SCAN_CHUNK),
                     lambda b, h, s: (jnp.where(s < nb, 1, 0), h, 0, 0, 0)),
    ]
    return pl.pallas_call(
        _scan_kernel,
        grid=(batch, RET_HEADS, 2 * nb),
        in_specs=in_specs,
        out_specs=pl.BlockSpec((SCAN_TC, RET_DV), lambda b, h, s: (b * nb + fwd_blk(s), h)),
        out_shape=jax.ShapeDtypeStruct((t, RET_VDIM), BF16),
        scratch_shapes=[pltpu.VMEM((RET_DK, RET_DV), F32), pltpu.VMEM((SEQ, RET_DV), F32)],
        compiler_params=_params(("arbitrary", "arbitrary", "arbitrary")),
        name="retention_scan",
    )(dec, qk, qk, v, gates, gates, tabs)


def _retention_tables(decay_f, decay_b):
    c = SCAN_CHUNK
    pos = jnp.arange(c, dtype=F32)
    ones = jnp.ones((1, 1, c), F32)

    def tables(log_gamma, backward):
        lg = log_gamma[:, None, None]
        diff = pos[:, None] - pos[None, :]
        if backward:
            diff = -diff
        inner = jnp.where(diff[None] >= 0, jnp.exp(lg * jnp.maximum(diff, 0.0)[None]), 0.0)
        cross_pos = (c - pos) if backward else (pos + 1.0)
        into_pos = pos if backward else (c - 1.0 - pos)
        cross = jnp.exp(log_gamma[:, None] * cross_pos[None])[:, :, None] * ones
        into = jnp.exp(log_gamma[:, None] * into_pos[None])[:, :, None] * ones
        return jnp.stack([inner, cross, into], axis=1), jnp.exp(log_gamma * c)

    lg_f = jnp.log1p(-jnp.exp(decay_f.astype(F32)))
    lg_b = jnp.log1p(-jnp.exp(decay_b.astype(F32)))
    tab_f, dec_f = tables(lg_f, False)
    tab_b, dec_b = tables(lg_b, True)
    return jnp.stack([tab_f, tab_b]), jnp.stack([dec_f, dec_b])


def _rotary_tables():
    d = RET_DK
    inv = ROPE_BASE ** (-jnp.arange(0, d, 2, dtype=F32) / d)
    ang = jnp.arange(SEQ, dtype=F32)[:, None] * inv[None, :]
    return jnp.cos(ang), jnp.sin(ang)


NA_QROWS = 4
NA_GROUPS = GRID_ROWS // NA_QROWS
NA_SLAB_ROWS = NA_QROWS + NA_KH
NA_QTOK = NA_QROWS * GRID_W
NA_SLAB = NA_SLAB_ROWS * GRID_W


NA_GEOMETRIES = (
    lambda rq, rk: (rk < NA_KH, rk - rq + NA_KH - 1),
    lambda rq, rk: (rq <= rk < rq + NA_KH, rk - rq + NA_KH // 2 - 1),
    lambda rq, rk: (rk >= NA_QROWS, rk - rq - 1),
)


def _na_kernel(q_ref, k_ref, v_ref, toe_ref, o_ref, s_ref, bias_ref):
    scale = NA_HEAD_DIM ** -0.5 * LOG2_E

    @pl.when(pl.program_id(1) == 0)
    def _():
        first_row = lax.broadcasted_iota(jnp.int32, (GRID_W, 2 * GRID_W), 1) < GRID_W
        masked = jnp.full((GRID_W, 2 * GRID_W), MASK_VALUE, F32)
        for kind, geometry in enumerate(NA_GEOMETRIES):
            for rq in range(NA_QROWS):
                for rk in range(0, NA_SLAB_ROWS, 2):
                    (ok0, dr0), (ok1, dr1) = geometry(rq, rk), geometry(rq, rk + 1)
                    tile = jnp.where(first_row, toe_ref[dr0] if ok0 else masked,
                                     toe_ref[dr1] if ok1 else masked) if (ok0 or ok1) else masked
                    bias_ref[kind, rq * GRID_W:(rq + 1) * GRID_W, rk * GRID_W:(rk + 2) * GRID_W] = tile

    def windows(g):
        s0 = jnp.clip(NA_QROWS * g - NA_KH // 2, 0, GRID_ROWS - NA_SLAB_ROWS)
        rows = pl.ds(pl.multiple_of(g * NA_QTOK, NA_QTOK), NA_QTOK)
        slab = pl.ds(pl.multiple_of(s0 * GRID_W, NA_QTOK), NA_SLAB)
        return rows, slab

    def scores(g, slot):
        rows, slab = windows(g)
        kind = jnp.where(g == 0, 0, jnp.where(g == NA_GROUPS - 1, 2, 1))
        s = lax.dot_general(q_ref[rows, :], k_ref[slab, :], (((1,), (1,)), ((), ())),
                            preferred_element_type=F32)
        s_ref[slot] = s * scale + bias_ref[kind]

    def attend(g, slot):
        rows, slab = windows(g)
        s = s_ref[slot]
        e = jnp.exp2(s - jnp.max(s, axis=-1, keepdims=True))
        o = jnp.dot(e.astype(BF16), v_ref[slab, :], preferred_element_type=F32)
        o_ref[rows, :] = (o * (1.0 / jnp.sum(e, axis=-1, keepdims=True))).astype(BF16)

    scores(0, 0)

    def pair(i, carry):
        g = 2 * i
        scores(g + 1, 1)
        attend(g, 0)
        scores(g + 2, 0)
        attend(g + 1, 1)
        return carry

    lax.fori_loop(0, NA_GROUPS // 2 - 1, pair, 0, unroll=True)
    scores(NA_GROUPS - 1, 1)
    attend(NA_GROUPS - 2, 0)
    attend(NA_GROUPS - 1, 1)


def _na_bias_tables(rpb):
    pad = GRID_W - NA_KW
    padded = jnp.pad(rpb.astype(F32), ((0, 0), (0, 0), (pad, pad)))
    toe = jnp.stack([padded[:, :, GRID_W - 1 - c: 2 * GRID_W - 1 - c] for c in range(GRID_W)], axis=2)
    cq = np.arange(GRID_W)
    ck = np.arange(GRID_W)
    qwin = np.clip(cq - NA_KW // 2, 0, GRID_W - NA_KW)
    col_ok = (ck[None, :] >= qwin[:, None]) & (ck[None, :] < qwin[:, None] + NA_KW)
    toe = jnp.where(col_ok[None, None], toe * LOG2_E, MASK_VALUE)
    return jnp.concatenate([toe, toe], axis=-1)


def _neighbourhood_attention(qkv, bias):
    t = qkv.shape[0]
    batch = t // SEQ
    hd = NA_HEAD_DIM
    return pl.pallas_call(
        _na_kernel,
        grid=(NA_HEADS, batch),
        in_specs=[
            pl.BlockSpec((SEQ, hd), lambda h, b: (b, h)),
            pl.BlockSpec((SEQ, hd), lambda h, b: (b, NA_HEADS + h)),
            pl.BlockSpec((SEQ, hd), lambda h, b: (b, 2 * NA_HEADS + h)),
            pl.BlockSpec((None, 2 * NA_KH - 1, GRID_W, 2 * GRID_W), lambda h, b: (h, 0, 0, 0)),
        ],
        out_specs=pl.BlockSpec((SEQ, hd), lambda h, b: (b, h)),
        out_shape=jax.ShapeDtypeStruct((t, D_MODEL), BF16),
        scratch_shapes=[pltpu.VMEM((2, NA_QTOK, NA_SLAB), F32), pltpu.VMEM((3, NA_QTOK, NA_SLAB), F32)],
        compiler_params=_params(("arbitrary", "arbitrary")),
        name="neighbourhood_attention",
    )(qkv, qkv, qkv, bias)


def _trunk(x, weights, tables, raw):
    assert DEPTH == 2

    def side(names):
        return [(raw[name], layer) for name, layer in names] if raw else []

    def publish(names, casted):
        for (name, layer), w in zip(names, casted):
            weights[name, layer] = (w, 0)

    for i in range(DEPTH):
        x = _ffn(x, tables["norm_ffn1"][i], *weights["ffn1_w_in", i], *weights["ffn1_w_out", i])
        j = i // 2
        if i % 2 == 0:
            w_in, lj = weights["ret_w_in", j]
            qk, h, _ = _proj(x, tables["norm_mix"][i], w_in, lj, (0, 2 * D_MODEL), rotary=tables["rotary"],
                             emit_h=True)
            names = [("ffn2_w_in", i), ("ffn2_w_out", i)]
            v, _, casted = _proj(h, None, w_in, lj, (2 * D_MODEL, 2 * D_MODEL + RET_VDIM), side=side(names))
            publish(names, casted)
            names = [("ffn1_w_in", i + 1), ("ffn1_w_out", i + 1), ("na_w_out", j)]
            gates, _, casted = _proj(h, None, w_in, lj, (2 * D_MODEL + RET_VDIM, RET_IN),
                                     gate_gain=jnp.concatenate([tables["ret_gn_f"][j], tables["ret_gn_b"][j]]),
                                     side=side(names))
            publish(names, casted)
            tabs, dec = tables["ret_tables"][j]
            mixed = _retention_scan(qk, v, gates, tabs, dec)
            names = [("ffn2_w_in", i + 1)]
            x, casted = _out_proj(mixed, *weights["ret_w_out", j], x, side=side(names))
            publish(names, casted)
        else:
            names = [("ffn2_w_out", i)]
            qkv, _, casted = _proj(x, tables["norm_mix"][i], *weights["na_w_in", j], (0, 3 * D_MODEL),
                                   side=side(names))
            publish(names, casted)
            att = _neighbourhood_attention(qkv, tables["na_bias"][j])
            x, _ = _out_proj(att, *weights["na_w_out", j], x)
        x = _ffn(x, tables["norm_ffn2"][i], *weights["ffn2_w_in", i], *weights["ffn2_w_out", i],
                 final_gain=tables["norm_final"] if i == DEPTH - 1 else None)
    return x


def kernel(x_prompt, x_sample, norm_ffn1, ffn1_w_in, ffn1_w_out, norm_mix, norm_ffn2, ffn2_w_in, ffn2_w_out,
           ret_w_in, ret_w_out, ret_decay_f, ret_decay_b, ret_gn_f, ret_gn_b, na_w_in, na_w_out, na_rpb,
           norm_final):
    n_ret = ret_w_in.shape[0]
    n_na = na_w_in.shape[0]
    tables = dict(
        norm_ffn1=norm_ffn1, norm_mix=norm_mix, norm_ffn2=norm_ffn2, norm_final=norm_final,
        ret_gn_f=ret_gn_f, ret_gn_b=ret_gn_b,
        ret_tables=[_retention_tables(ret_decay_f[j], ret_decay_b[j]) for j in range(n_ret)],
        rotary=_rotary_tables(),
        na_bias=[_na_bias_tables(na_rpb[j]) for j in range(n_na)],
    )
    weights = {}
    up_front = dict(ffn1_w_in=ffn1_w_in[:1], ffn1_w_out=ffn1_w_out[:1], ret_w_in=ret_w_in, ret_w_out=ret_w_out,
                    na_w_in=na_w_in)
    for name, w in up_front.items():
        stack = w.astype(BF16)
        for layer in range(w.shape[0]):
            weights[name, layer] = (stack, layer)
    raw = dict(ffn1_w_in=ffn1_w_in, ffn1_w_out=ffn1_w_out, ffn2_w_in=ffn2_w_in, ffn2_w_out=ffn2_w_out,
               na_w_out=na_w_out)
    outs = []
    for x in (x_prompt, x_sample):
        y = _trunk(x.reshape(-1, D_MODEL), weights, tables, raw)
        raw = None
        outs.append(y.reshape(x.shape))
    return tuple(outs)
```

```python
import functools

import numpy as np
import jax
import jax.numpy as jnp
from jax import lax
from jax.experimental import pallas as pl
from jax.experimental.pallas import tpu as pltpu

D_MODEL = 2048
SEQ = 4096
DEPTH = 2
GRID_W = 64
D_FF = 5632
RET_HEADS = 8
RET_DK = D_MODEL // RET_HEADS
RET_DV = 2 * RET_DK
RET_VDIM = RET_HEADS * RET_DV
RET_IN = 2 * D_MODEL + 3 * RET_VDIM
ROPE_BASE = 10000.0
NA_HEADS = 16
NA_HEAD_DIM = D_MODEL // NA_HEADS
NA_KH = 8
NA_KW = 16
RMS_EPS = 1e-6
MASK_VALUE = -1e30
LOG2_E = 1.4426950408889634

F32 = jnp.float32
BF16 = jnp.bfloat16

V7X_VMEM_BYTES = 64 * 1024 * 1024
VMEM_LIMIT_BYTES = V7X_VMEM_BYTES - 8 * 1024 * 1024

FFN_TM, FFN_TF = 1024, 512
PROJ_TM, PROJ_TN = 1024, 2048
OUT_TM = 512
SCAN_TC = 2048
SCAN_CHUNK = 256
GRID_ROWS = SEQ // GRID_W


def _params(semantics):
    return pltpu.CompilerParams(dimension_semantics=semantics, vmem_limit_bytes=VMEM_LIMIT_BYTES)


def _rmsnorm(xf, g):
    ms = jnp.mean(xf * xf, axis=-1, keepdims=True)
    return xf * lax.rsqrt(ms + RMS_EPS) * g


def _silu(g):
    return g / (1.0 + jnp.exp(-g))


def _ffn_kernel(x_ref, g_ref, wg_ref, wu_ref, wo_ref, *rest, final_norm):
    if final_norm:
        gfin_ref, o_ref, h_ref = rest
    else:
        o_ref, h_ref = rest
    j = pl.program_id(1)

    @pl.when(j == 0)
    def _():
        x = x_ref[...]
        h_ref[...] = _rmsnorm(x, g_ref[...]).astype(BF16)
        o_ref[...] = x

    h = h_ref[...]
    g = jnp.dot(h, wg_ref[...], preferred_element_type=F32)
    u = jnp.dot(h, wu_ref[...], preferred_element_type=F32)
    a = (_silu(g) * u * 0.5).astype(BF16)
    o_ref[...] += jnp.dot(a, wo_ref[...], preferred_element_type=F32)

    if final_norm:
        @pl.when(j == pl.num_programs(1) - 1)
        def _():
            o_ref[...] = _rmsnorm(o_ref[...], gfin_ref[...])


def _ffn(x, gain, w_in, layer_in, w_out, layer_out, final_gain=None):
    t = x.shape[0]
    nj = D_FF // FFN_TF
    in_specs = [
        pl.BlockSpec((FFN_TM, D_MODEL), lambda i, j: (i, 0)),
        pl.BlockSpec((1, D_MODEL), lambda i, j: (0, 0)),
        pl.BlockSpec((None, D_MODEL, FFN_TF), lambda i, j: (layer_in, 0, j)),
        pl.BlockSpec((None, D_MODEL, FFN_TF), lambda i, j: (layer_in, 0, nj + j)),
        pl.BlockSpec((None, FFN_TF, D_MODEL), lambda i, j: (layer_out, j, 0)),
    ]
    args = [x, gain.reshape(1, D_MODEL), w_in, w_in, w_out]
    if final_gain is not None:
        in_specs.append(pl.BlockSpec((1, D_MODEL), lambda i, j: (0, 0)))
        args.append(final_gain.reshape(1, D_MODEL))
    return pl.pallas_call(
        functools.partial(_ffn_kernel, final_norm=final_gain is not None),
        grid=(t // FFN_TM, nj),
        in_specs=in_specs,
        out_specs=pl.BlockSpec((FFN_TM, D_MODEL), lambda i, j: (i, 0)),
        out_shape=jax.ShapeDtypeStruct((t, D_MODEL), F32),
        scratch_shapes=[pltpu.VMEM((FFN_TM, D_MODEL), BF16)],
        compiler_params=_params(("parallel", "arbitrary")),
        name="swiglu_ffn",
    )(*args)


SIDE_CHUNKS = 32


def _side_cast_specs(jobs, step_of):
    in_specs, out_specs, out_shapes, args = [], [], [], []
    for w, layer in jobs:
        _, r, c = w.shape
        rows = r // SIDE_CHUNKS
        assert rows * SIDE_CHUNKS == r and rows % 16 == 0

        def chunk(*g):
            return jnp.minimum(step_of(*g), SIDE_CHUNKS - 1)

        in_specs.append(pl.BlockSpec((None, rows, c), lambda *g, layer=layer: (layer, chunk(*g), 0)))
        out_specs.append(pl.BlockSpec((None, rows, c), lambda *g: (0, chunk(*g), 0)))
        out_shapes.append(jax.ShapeDtypeStruct((1, r, c), BF16))
        args.append(w)
    return in_specs, out_specs, out_shapes, args


def _side_cast(srcs, dsts):
    for src, dst in zip(srcs, dsts):
        dst[...] = src[...].astype(BF16)


def _proj_kernel(*refs, epilogue, h_mode, n_side):
    refs = list(refs)
    if h_mode == "input":
        h_ref = refs.pop(0)
    else:
        x_ref, g_ref = refs.pop(0), refs.pop(0)
    w_ref = refs.pop(0)
    if epilogue == "rotary":
        cos_ref, sin_ref = refs.pop(0), refs.pop(0)
    elif epilogue == "gate":
        gn_ref = refs.pop(0)
    side_srcs = [refs.pop(0) for _ in range(n_side)]
    o_ref = refs.pop(0)
    if h_mode == "emit":
        h_ref = refs.pop(0)
    side_dsts = [refs.pop(0) for _ in range(n_side)]
    if h_mode == "scratch":
        h_ref = refs.pop(0)
    n = pl.program_id(1)
    _side_cast(side_srcs, side_dsts)

    if h_mode != "input":
        @pl.when(n == 0)
        def _():
            h_ref[...] = _rmsnorm(x_ref[...], g_ref[...]).astype(BF16)

    y = jnp.dot(h_ref[...], w_ref[...], preferred_element_type=F32)
    if epilogue == "plain":
        o_ref[...] = y.astype(BF16)
        return
    if epilogue == "gate":
        o_ref[...] = (_silu(y) * gn_ref[...]).astype(BF16)
        return
    cos = cos_ref[...]
    sin = sin_ref[...]
    scale = jnp.where(n < pl.num_programs(1) // 2, 1.0, RET_DK ** -0.5).astype(F32)
    half = RET_DK // 2
    for hh in range(PROJ_TN // RET_DK):
        x1 = y[:, hh * RET_DK: hh * RET_DK + half]
        x2 = y[:, hh * RET_DK + half: (hh + 1) * RET_DK]
        o_ref[:, hh * RET_DK: hh * RET_DK + half] = ((x1 * cos - x2 * sin) * scale).astype(BF16)
        o_ref[:, hh * RET_DK + half: (hh + 1) * RET_DK] = ((x1 * sin + x2 * cos) * scale).astype(BF16)


def _proj(x, gain, w, layer, cols, rotary=None, gate_gain=None, emit_h=False, side=()):
    t = x.shape[0]
    n_out = cols[1] - cols[0]
    col0 = cols[0] // PROJ_TN
    grid = (t // PROJ_TM, n_out // PROJ_TN)
    row_spec = pl.BlockSpec((PROJ_TM, D_MODEL), lambda i, n: (i, 0))
    h_mode = "input" if gain is None else ("emit" if emit_h else "scratch")
    in_specs = [row_spec]
    args = [x]
    if gain is not None:
        in_specs.append(pl.BlockSpec((1, D_MODEL), lambda i, n: (0, 0)))
        args.append(gain.reshape(1, D_MODEL))
    in_specs.append(pl.BlockSpec((None, D_MODEL, PROJ_TN), lambda i, n: (layer, 0, col0 + n)))
    args.append(w)
    epilogue = "plain"
    if rotary is not None:
        epilogue = "rotary"
        seq_tiles = SEQ // PROJ_TM
        tab_spec = pl.BlockSpec((PROJ_TM, RET_DK // 2), lambda i, n: (i % seq_tiles, 0))
        in_specs += [tab_spec, tab_spec]
        args += list(rotary)
    elif gate_gain is not None:
        epilogue = "gate"
        in_specs.append(pl.BlockSpec((1, PROJ_TN), lambda i, n: (0, n)))
        args.append(gate_gain.reshape(1, n_out).astype(F32))
    assert not side or grid[0] * grid[1] >= SIDE_CHUNKS
    s_in, s_out, s_shapes, s_args = _side_cast_specs(side, lambda i, n: i * grid[1] + n)
    out_specs = [pl.BlockSpec((PROJ_TM, PROJ_TN), lambda i, n: (i, n))]
    out_shape = [jax.ShapeDtypeStruct((t, n_out), BF16)]
    if h_mode == "emit":
        out_specs.append(row_spec)
        out_shape.append(jax.ShapeDtypeStruct((t, D_MODEL), BF16))
    outs = pl.pallas_call(
        functools.partial(_proj_kernel, epilogue=epilogue, h_mode=h_mode, n_side=len(side)),
        grid=grid,
        in_specs=in_specs + s_in,
        out_specs=out_specs + s_out,
        out_shape=out_shape + s_shapes,
        scratch_shapes=[pltpu.VMEM((PROJ_TM, D_MODEL), BF16)] if h_mode == "scratch" else [],
        compiler_params=_params(("parallel", "arbitrary")),
        name="proj_" + epilogue + "_h" + h_mode,
    )(*args, *s_args)
    n_main = len(out_specs)
    return outs[0], (outs[1] if h_mode == "emit" else None), list(outs[n_main:])


def _out_kernel(a_ref, w_ref, x_ref, *rest, n_side):
    side_srcs, o_ref, side_dsts = rest[:n_side], rest[n_side], rest[n_side + 1:]
    _side_cast(side_srcs, side_dsts)
    o_ref[...] = x_ref[...] + jnp.dot(a_ref[...], w_ref[...], preferred_element_type=F32)


def _out_proj(a, w, layer, x, side=()):
    t, k = a.shape
    steps = t // OUT_TM
    assert not side or steps >= SIDE_CHUNKS
    s_in, s_out, s_shapes, s_args = _side_cast_specs(side, lambda i: i)
    outs = pl.pallas_call(
        functools.partial(_out_kernel, n_side=len(side)),
        grid=(steps,),
        in_specs=[
            pl.BlockSpec((OUT_TM, k), lambda i: (i, 0)),
            pl.BlockSpec((None, k, D_MODEL), lambda i: (layer, 0, 0), pipeline_mode=pl.Buffered(1)),
            pl.BlockSpec((OUT_TM, D_MODEL), lambda i: (i, 0)),
        ] + s_in,
        out_specs=[pl.BlockSpec((OUT_TM, D_MODEL), lambda i: (i, 0))] + s_out,
        out_shape=[jax.ShapeDtypeStruct((t, D_MODEL), F32)] + s_shapes,
        compiler_params=_params(("parallel",)),
        name="out_proj_residual",
    )(a, w, x, *s_args)
    return outs[0], list(outs[1:])


SCAN_NBLK = SEQ // SCAN_TC
SCAN_CHUNKS = SCAN_TC // SCAN_CHUNK


def _scan_kernel(dec_ref, q_ref, k_ref, v_ref, gf_ref, gb_ref, tab_ref, o_ref, r_ref, ob_ref):
    h = pl.program_id(1)
    t = pl.program_id(2)
    c_len = SCAN_CHUNK

    @pl.when((t == 0) | (t == SCAN_NBLK))
    def _():
        r_ref[...] = jnp.zeros_like(r_ref)

    def run(backward):
        inner = tab_ref[0]
        cross = jnp.concatenate([tab_ref[1]] * (RET_DV // c_len), axis=1)
        into = jnp.concatenate([tab_ref[2]] * (RET_DK // c_len), axis=1)
        dec = dec_ref[1 if backward else 0, h]
        gate_ref = gb_ref if backward else gf_ref
        blk = (SCAN_NBLK - 1 - t) if backward else (t - SCAN_NBLK)
        order = range(SCAN_CHUNKS - 1, -1, -1) if backward else range(SCAN_CHUNKS)
        for c in order:
            rows = pl.ds(c * c_len, c_len)
            seq_rows = pl.ds(pl.multiple_of(blk * SCAN_TC + c * c_len, c_len), c_len)
            q = q_ref[rows, :]
            k = k_ref[rows, :]
            v = v_ref[rows, :]
            s = lax.dot_general(q, k, (((1,), (1,)), ((), ())), preferred_element_type=F32) * inner
            kd = (k.astype(F32) * into).astype(BF16)
            r = r_ref[...]
            sv = jnp.dot(jnp.concatenate([s.astype(BF16), kd.T], axis=0), v, preferred_element_type=F32)
            o = sv[:c_len] + jnp.dot(q, r.astype(BF16), preferred_element_type=F32) * cross
            r_ref[...] = r * dec + sv[c_len:]
            on = o * lax.rsqrt(jnp.mean(o * o, axis=-1, keepdims=True) + RMS_EPS)
            m = gate_ref[rows, :].astype(F32) * on
            if backward:
                ob_ref[seq_rows, :] = m
            else:
                o_ref[rows, :] = (m + ob_ref[seq_rows, :]).astype(BF16)

    @pl.when(t < SCAN_NBLK)
    def _():
        o_ref[...] = jnp.zeros_like(o_ref)
        run(True)

    @pl.when(t >= SCAN_NBLK)
    def _():
        run(False)


def _retention_scan(qk, v, gates, tabs, dec):
    t = qk.shape[0]
    batch = t // SEQ
    nb = SCAN_NBLK

    def bwd_blk(s):
        return jnp.where(s < nb, nb - 1 - s, 0)

    def fwd_blk(s):
        return jnp.where(s < nb, 0, s - nb)

    def cur_blk(s):
        return jnp.where(s < nb, nb - 1 - s, s - nb)

    nh = RET_HEADS
    in_specs = [
        pl.BlockSpec(memory_space=pltpu.SMEM),
        pl.BlockSpec((SCAN_TC, RET_DK), lambda b, h, s: (b * nb + cur_blk(s), h)),
        pl.BlockSpec((SCAN_TC, RET_DK), lambda b, h, s: (b * nb + cur_blk(s), nh + h)),
        pl.BlockSpec((SCAN_TC, RET_DV), lambda b, h, s: (b * nb + cur_blk(s), h)),
        pl.BlockSpec((SCAN_TC, RET_DV), lambda b, h, s: (b * nb + fwd_blk(s), h)),
        pl.BlockSpec((SCAN_TC, RET_DV), lambda b, h, s: (b * nb + bwd_blk(s), nh + h)),
        pl.BlockSpec((None, None, 3, SCAN_CHUNK, SCAN_CHUNK),
                     lambda b, h, s: (jnp.where(s < nb, 1, 0), h, 0, 0, 0)),
    ]
    return pl.pallas_call(
        _scan_kernel,
        grid=(batch, RET_HEADS, 2 * nb),
        in_specs=in_specs,
        out_specs=pl.BlockSpec((SCAN_TC, RET_DV), lambda b, h, s: (b * nb + fwd_blk(s), h)),
        out_shape=jax.ShapeDtypeStruct((t, RET_VDIM), BF16),
        scratch_shapes=[pltpu.VMEM((RET_DK, RET_DV), F32), pltpu.VMEM((SEQ, RET_DV), F32)],
        compiler_params=_params(("arbitrary", "arbitrary", "arbitrary")),
        name="retention_scan",
    )(dec, qk, qk, v, gates, gates, tabs)


def _retention_tables(decay_f, decay_b):
    c = SCAN_CHUNK
    pos = jnp.arange(c, dtype=F32)
    ones = jnp.ones((1, 1, c), F32)

    def tables(log_gamma, backward):
        lg = log_gamma[:, None, None]
        diff = pos[:, None] - pos[None, :]
        if backward:
            diff = -diff
        inner = jnp.where(diff[None] >= 0, jnp.exp(lg * jnp.maximum(diff, 0.0)[None]), 0.0)
        cross_pos = (c - pos) if backward else (pos + 1.0)
        into_pos = pos if backward else (c - 1.0 - pos)
        cross = jnp.exp(log_gamma[:, None] * cross_pos[None])[:, :, None] * ones
        into = jnp.exp(log_gamma[:, None] * into_pos[None])[:, :, None] * ones
        return jnp.stack([inner, cross, into], axis=1), jnp.exp(log_gamma * c)

    lg_f = jnp.log1p(-jnp.exp(decay_f.astype(F32)))
    lg_b = jnp.log1p(-jnp.exp(decay_b.astype(F32)))
    tab_f, dec_f = tables(lg_f, False)
    tab_b, dec_b = tables(lg_b, True)
    return jnp.stack([tab_f, tab_b]), jnp.stack([dec_f, dec_b])


def _rotary_tables():
    d = RET_DK
    inv = ROPE_BASE ** (-jnp.arange(0, d, 2, dtype=F32) / d)
    ang = jnp.arange(SEQ, dtype=F32)[:, None] * inv[None, :]
    return jnp.cos(ang), jnp.sin(ang)


NA_QROWS = 4
NA_GROUPS = GRID_ROWS // NA_QROWS
NA_SLAB_ROWS = NA_QROWS + NA_KH
NA_QTOK = NA_QROWS * GRID_W
NA_SLAB = NA_SLAB_ROWS * GRID_W


NA_GEOMETRIES = (
    lambda rq, rk: (rk < NA_KH, rk - rq + NA_KH - 1),
    lambda rq, rk: (rq <= rk < rq + NA_KH, rk - rq + NA_KH // 2 - 1),
    lambda rq, rk: (rk >= NA_QROWS, rk - rq - 1),
)


def _na_kernel(q_ref, k_ref, v_ref, toe_ref, o_ref, s_ref, bias_ref):
    scale = NA_HEAD_DIM ** -0.5 * LOG2_E

    @pl.when(pl.program_id(1) == 0)
    def _():
        first_row = lax.broadcasted_iota(jnp.int32, (GRID_W, 2 * GRID_W), 1) < GRID_W
        masked = jnp.full((GRID_W, 2 * GRID_W), MASK_VALUE, F32)
        for kind, geometry in enumerate(NA_GEOMETRIES):
            for rq in range(NA_QROWS):
                for rk in range(0, NA_SLAB_ROWS, 2):
                    (ok0, dr0), (ok1, dr1) = geometry(rq, rk), geometry(rq, rk + 1)
                    tile = jnp.where(first_row, toe_ref[dr0] if ok0 else masked,
                                     toe_ref[dr1] if ok1 else masked) if (ok0 or ok1) else masked
                    bias_ref[kind, rq * GRID_W:(rq + 1) * GRID_W, rk * GRID_W:(rk + 2) * GRID_W] = tile

    def windows(g):
        s0 = jnp.clip(NA_QROWS * g - NA_KH // 2, 0, GRID_ROWS - NA_SLAB_ROWS)
        rows = pl.ds(pl.multiple_of(g * NA_QTOK, NA_QTOK), NA_QTOK)
        slab = pl.ds(pl.multiple_of(s0 * GRID_W, NA_QTOK), NA_SLAB)
        return rows, slab

    def scores(g, slot):
        rows, slab = windows(g)
        kind = jnp.where(g == 0, 0, jnp.where(g == NA_GROUPS - 1, 2, 1))
        s = lax.dot_general(q_ref[rows, :], k_ref[slab, :], (((1,), (1,)), ((), ())),
                            preferred_element_type=F32)
        s_ref[slot] = s * scale + bias_ref[kind]

    def attend(g, slot):
        rows, slab = windows(g)
        s = s_ref[slot]
        e = jnp.exp2(s - jnp.max(s, axis=-1, keepdims=True))
        o = jnp.dot(e.astype(BF16), v_ref[slab, :], preferred_element_type=F32)
        o_ref[rows, :] = (o * (1.0 / jnp.sum(e, axis=-1, keepdims=True))).astype(BF16)

    scores(0, 0)

    def pair(i, carry):
        g = 2 * i
        scores(g + 1, 1)
        attend(g, 0)
        scores(g + 2, 0)
        attend(g + 1, 1)
        return carry

    lax.fori_loop(0, NA_GROUPS // 2 - 1, pair, 0, unroll=True)
    scores(NA_GROUPS - 1, 1)
    attend(NA_GROUPS - 2, 0)
    attend(NA_GROUPS - 1, 1)


def _na_bias_tables(rpb):
    pad = GRID_W - NA_KW
    padded = jnp.pad(rpb.astype(F32), ((0, 0), (0, 0), (pad, pad)))
    toe = jnp.stack([padded[:, :, GRID_W - 1 - c: 2 * GRID_W - 1 - c] for c in range(GRID_W)], axis=2)
    cq = np.arange(GRID_W)
    ck = np.arange(GRID_W)
    qwin = np.clip(cq - NA_KW // 2, 0, GRID_W - NA_KW)
    col_ok = (ck[None, :] >= qwin[:, None]) & (ck[None, :] < qwin[:, None] + NA_KW)
    toe = jnp.where(col_ok[None, None], toe * LOG2_E, MASK_VALUE)
    return jnp.concatenate([toe, toe], axis=-1)


def _neighbourhood_attention(qkv, bias):
    t = qkv.shape[0]
    batch = t // SEQ
    hd = NA_HEAD_DIM
    return pl.pallas_call(
        _na_kernel,
        grid=(NA_HEADS, batch),
        in_specs=[
            pl.BlockSpec((SEQ, hd), lambda h, b: (b, h)),
            pl.BlockSpec((SEQ, hd), lambda h, b: (b, NA_HEADS + h)),
            pl.BlockSpec((SEQ, hd), lambda h, b: (b, 2 * NA_HEADS + h)),
            pl.BlockSpec((None, 2 * NA_KH - 1, GRID_W, 2 * GRID_W), lambda h, b: (h, 0, 0, 0)),
        ],
        out_specs=pl.BlockSpec((SEQ, hd), lambda h, b: (b, h)),
        out_shape=jax.ShapeDtypeStruct((t, D_MODEL), BF16),
        scratch_shapes=[pltpu.VMEM((2, NA_QTOK, NA_SLAB), F32), pltpu.VMEM((3, NA_QTOK, NA_SLAB), F32)],
        compiler_params=_params(("arbitrary", "arbitrary")),
        name="neighbourhood_attention",
    )(qkv, qkv, qkv, bias)


def _trunk(x, weights, tables, raw):
    assert DEPTH == 2

    def side(names):
        return [(raw[name], layer) for name, layer in names] if raw else []

    def publish(names, casted):
        for (name, layer), w in zip(names, casted):
            weights[name, layer] = (w, 0)

    for i in range(DEPTH):
        x = _ffn(x, tables["norm_ffn1"][i], *weights["ffn1_w_in", i], *weights["ffn1_w_out", i])
        j = i // 2
        if i % 2 == 0:
            w_in, lj = weights["ret_w_in", j]
            qk, h, _ = _proj(x, tables["norm_mix"][i], w_in, lj, (0, 2 * D_MODEL), rotary=tables["rotary"],
                             emit_h=True)
            names = [("ffn2_w_in", i), ("ffn2_w_out", i), ("na_w_in", j)]
            v, _, casted = _proj(h, None, w_in, lj, (2 * D_MODEL, 2 * D_MODEL + RET_VDIM), side=side(names))
            publish(names, casted)
            names = [("ffn1_w_in", i + 1), ("ffn1_w_out", i + 1), ("na_w_out", j), ("ret_w_out", j)]
            gates, _, casted = _proj(h, None, w_in, lj, (2 * D_MODEL + RET_VDIM, RET_IN),
                                     gate_gain=jnp.concatenate([tables["ret_gn_f"][j], tables["ret_gn_b"][j]]),
                                     side=side(names))
            publish(names, casted)
            tabs, dec = tables["ret_tables"][j]
            mixed = _retention_scan(qk, v, gates, tabs, dec)
            names = [("ffn2_w_in", i + 1)]
            x, casted = _out_proj(mixed, *weights["ret_w_out", j], x, side=side(names))
            publish(names, casted)
        else:
            names = [("ffn2_w_out", i)]
            qkv, _, casted = _proj(x, tables["norm_mix"][i], *weights["na_w_in", j], (0, 3 * D_MODEL),
                                   side=side(names))
            publish(names, casted)
            att = _neighbourhood_attention(qkv, tables["na_bias"][j])
            x, _ = _out_proj(att, *weights["na_w_out", j], x)
        x = _ffn(x, tables["norm_ffn2"][i], *weights["ffn2_w_in", i], *weights["ffn2_w_out", i],
                 final_gain=tables["norm_final"] if i == DEPTH - 1 else None)
    return x


def kernel(x_prompt, x_sample, norm_ffn1, ffn1_w_in, ffn1_w_out, norm_mix, norm_ffn2, ffn2_w_in, ffn2_w_out,
           ret_w_in, ret_w_out, ret_decay_f, ret_decay_b, ret_gn_f, ret_gn_b, na_w_in, na_w_out, na_rpb,
           norm_final):
    n_ret = ret_w_in.shape[0]
    n_na = na_w_in.shape[0]
    tables = dict(
        norm_ffn1=norm_ffn1, norm_mix=norm_mix, norm_ffn2=norm_ffn2, norm_final=norm_final,
        ret_gn_f=ret_gn_f, ret_gn_b=ret_gn_b,
        ret_tables=[_retention_tables(ret_decay_f[j], ret_decay_b[j]) for j in range(n_ret)],
        rotary=_rotary_tables(),
        na_bias=[_na_bias_tables(na_rpb[j]) for j in range(n_na)],
    )
    weights = {}
    up_front = dict(ffn1_w_in=ffn1_w_in[:1], ffn1_w_out=ffn1_w_out[:1], ret_w_in=ret_w_in)
    for name, w in up_front.items():
        stack = w.astype(BF16)
        for layer in range(w.shape[0]):
            weights[name, layer] = (stack, layer)
    raw = dict(ffn1_w_in=ffn1_w_in, ffn1_w_out=ffn1_w_out, ffn2_w_in=ffn2_w_in, ffn2_w_out=ffn2_w_out,
               ret_w_out=ret_w_out, na_w_in=na_w_in, na_w_out=na_w_out)
    outs = []
    for x in (x_prompt, x_sample):
        y = _trunk(x.reshape(-1, D_MODEL), weights, tables, raw)
        raw = None
        outs.append(y.reshape(x.shape))
    return tuple(outs)
```

```python
import functools

import numpy as np
import jax
import jax.numpy as jnp
from jax import lax
from jax.experimental import pallas as pl
from jax.experimental.pallas import tpu as pltpu

D_MODEL = 2048
SEQ = 4096
DEPTH = 2
GRID_W = 64
D_FF = 5632
RET_HEADS = 8
RET_DK = D_MODEL // RET_HEADS
RET_DV = 2 * RET_DK
RET_VDIM = RET_HEADS * RET_DV
RET_IN = 2 * D_MODEL + 3 * RET_VDIM
ROPE_BASE = 10000.0
NA_HEADS = 16
NA_HEAD_DIM = D_MODEL // NA_HEADS
NA_KH = 8
NA_KW = 16
RMS_EPS = 1e-6
MASK_VALUE = -1e30
LOG2_E = 1.4426950408889634

F32 = jnp.float32
BF16 = jnp.bfloat16

V7X_VMEM_BYTES = 64 * 1024 * 1024
VMEM_LIMIT_BYTES = V7X_VMEM_BYTES - 8 * 1024 * 1024

FFN_TM, FFN_TF = 1024, 512
PROJ_TM, PROJ_TN = 1024, 2048
OUT_TM = 512
SCAN_TC = 2048
SCAN_CHUNK = 256
GRID_ROWS = SEQ // GRID_W


def _params(semantics):
    return pltpu.CompilerParams(dimension_semantics=semantics, vmem_limit_bytes=VMEM_LIMIT_BYTES)


def _rmsnorm(xf, g):
    ms = jnp.mean(xf * xf, axis=-1, keepdims=True)
    return xf * lax.rsqrt(ms + RMS_EPS) * g


def _silu(g):
    return g / (1.0 + jnp.exp(-g))


def _ffn_kernel(x_ref, g_ref, wgu_ref, wo_ref, *rest, final_norm):
    if final_norm:
        gfin_ref, o_ref, h_ref = rest
    else:
        o_ref, h_ref = rest
    j = pl.program_id(1)

    @pl.when(j == 0)
    def _():
        x = x_ref[...]
        h_ref[...] = _rmsnorm(x, g_ref[...]).astype(BF16)
        o_ref[...] = x

    h = h_ref[...]
    gu = jnp.dot(h, wgu_ref[...], preferred_element_type=F32)
    g, u = gu[:, :FFN_TF], gu[:, FFN_TF:]
    a = (_silu(g) * u * 0.5).astype(BF16)
    o_ref[...] += jnp.dot(a, wo_ref[...], preferred_element_type=F32)

    if final_norm:
        @pl.when(j == pl.num_programs(1) - 1)
        def _():
            o_ref[...] = _rmsnorm(o_ref[...], gfin_ref[...])


def _ffn(x, gain, w_in, layer_in, w_out, layer_out, final_gain=None):
    t = x.shape[0]
    nj = D_FF // FFN_TF
    in_specs = [
        pl.BlockSpec((FFN_TM, D_MODEL), lambda i, j: (i, 0)),
        pl.BlockSpec((1, D_MODEL), lambda i, j: (0, 0)),
        pl.BlockSpec((None, D_MODEL, 2 * FFN_TF), lambda i, j: (layer_in, 0, j)),
        pl.BlockSpec((None, FFN_TF, D_MODEL), lambda i, j: (layer_out, j, 0)),
    ]
    args = [x, gain.reshape(1, D_MODEL), w_in, w_out]
    if final_gain is not None:
        in_specs.append(pl.BlockSpec((1, D_MODEL), lambda i, j: (0, 0)))
        args.append(final_gain.reshape(1, D_MODEL))
    return pl.pallas_call(
        functools.partial(_ffn_kernel, final_norm=final_gain is not None),
        grid=(t // FFN_TM, nj),
        in_specs=in_specs,
        out_specs=pl.BlockSpec((FFN_TM, D_MODEL), lambda i, j: (i, 0)),
        out_shape=jax.ShapeDtypeStruct((t, D_MODEL), F32),
        scratch_shapes=[pltpu.VMEM((FFN_TM, D_MODEL), BF16)],
        compiler_params=_params(("parallel", "arbitrary")),
        name="swiglu_ffn",
    )(*args)


SIDE_CHUNKS = 32


def _side_cast_specs(jobs, step_of):
    in_specs, out_specs, out_shapes, args = [], [], [], []
    for w, layer, _ in jobs:
        _, r, c = w.shape
        rows = r // SIDE_CHUNKS
        assert rows * SIDE_CHUNKS == r and rows % 16 == 0

        def chunk(*g):
            return jnp.minimum(step_of(*g), SIDE_CHUNKS - 1)

        in_specs.append(pl.BlockSpec((None, rows, c), lambda *g, layer=layer: (layer, chunk(*g), 0)))
        out_specs.append(pl.BlockSpec((None, rows, c), lambda *g: (0, chunk(*g), 0)))
        out_shapes.append(jax.ShapeDtypeStruct((1, r, c), BF16))
        args.append(w)
    return in_specs, out_specs, out_shapes, args


def _pair_gate_up(w):
    nj = D_FF // FFN_TF
    return w.reshape(*w.shape[:-1], 2, nj, FFN_TF).swapaxes(-3, -2).reshape(w.shape)


def _side_cast(srcs, dsts, paired):
    for src, dst, pair in zip(srcs, dsts, paired):
        if not pair:
            dst[...] = src[...].astype(BF16)
            continue
        for j in range(D_FF // FFN_TF):
            for part in range(2):
                col = part * D_FF + j * FFN_TF
                dst[:, (2 * j + part) * FFN_TF:(2 * j + part + 1) * FFN_TF] = src[:, col:col + FFN_TF].astype(BF16)


def _proj_kernel(*refs, epilogue, h_mode, side_paired):
    refs = list(refs)
    if h_mode == "input":
        h_ref = refs.pop(0)
    else:
        x_ref, g_ref = refs.pop(0), refs.pop(0)
    w_ref = refs.pop(0)
    if epilogue == "rotary":
        cos_ref, sin_ref = refs.pop(0), refs.pop(0)
    elif epilogue == "gate":
        gn_ref = refs.pop(0)
    side_srcs = [refs.pop(0) for _ in side_paired]
    o_ref = refs.pop(0)
    if h_mode == "emit":
        h_ref = refs.pop(0)
    side_dsts = [refs.pop(0) for _ in side_paired]
    if h_mode == "scratch":
        h_ref = refs.pop(0)
    n = pl.program_id(1)
    _side_cast(side_srcs, side_dsts, side_paired)

    if h_mode != "input":
        @pl.when(n == 0)
        def _():
            h_ref[...] = _rmsnorm(x_ref[...], g_ref[...]).astype(BF16)

    y = jnp.dot(h_ref[...], w_ref[...], preferred_element_type=F32)
    if epilogue == "plain":
        o_ref[...] = y.astype(BF16)
        return
    if epilogue == "gate":
        o_ref[...] = (_silu(y) * gn_ref[...]).astype(BF16)
        return
    cos = cos_ref[...]
    sin = sin_ref[...]
    scale = jnp.where(n < pl.num_programs(1) // 2, 1.0, RET_DK ** -0.5).astype(F32)
    half = RET_DK // 2
    for hh in range(PROJ_TN // RET_DK):
        x1 = y[:, hh * RET_DK: hh * RET_DK + half]
        x2 = y[:, hh * RET_DK + half: (hh + 1) * RET_DK]
        o_ref[:, hh * RET_DK: hh * RET_DK + half] = ((x1 * cos - x2 * sin) * scale).astype(BF16)
        o_ref[:, hh * RET_DK + half: (hh + 1) * RET_DK] = ((x1 * sin + x2 * cos) * scale).astype(BF16)


def _proj(x, gain, w, layer, cols, rotary=None, gate_gain=None, emit_h=False, side=()):
    t = x.shape[0]
    n_out = cols[1] - cols[0]
    col0 = cols[0] // PROJ_TN
    grid = (t // PROJ_TM, n_out // PROJ_TN)
    row_spec = pl.BlockSpec((PROJ_TM, D_MODEL), lambda i, n: (i, 0))
    h_mode = "input" if gain is None else ("emit" if emit_h else "scratch")
    in_specs = [row_spec]
    args = [x]
    if gain is not None:
        in_specs.append(pl.BlockSpec((1, D_MODEL), lambda i, n: (0, 0)))
        args.append(gain.reshape(1, D_MODEL))
    in_specs.append(pl.BlockSpec((None, D_MODEL, PROJ_TN), lambda i, n: (layer, 0, col0 + n)))
    args.append(w)
    epilogue = "plain"
    if rotary is not None:
        epilogue = "rotary"
        seq_tiles = SEQ // PROJ_TM
        tab_spec = pl.BlockSpec((PROJ_TM, RET_DK // 2), lambda i, n: (i % seq_tiles, 0))
        in_specs += [tab_spec, tab_spec]
        args += list(rotary)
    elif gate_gain is not None:
        epilogue = "gate"
        in_specs.append(pl.BlockSpec((1, PROJ_TN), lambda i, n: (0, n)))
        args.append(gate_gain.reshape(1, n_out).astype(F32))
    assert not side or grid[0] * grid[1] >= SIDE_CHUNKS
    s_in, s_out, s_shapes, s_args = _side_cast_specs(side, lambda i, n: i * grid[1] + n)
    out_specs = [pl.BlockSpec((PROJ_TM, PROJ_TN), lambda i, n: (i, n))]
    out_shape = [jax.ShapeDtypeStruct((t, n_out), BF16)]
    if h_mode == "emit":
        out_specs.append(row_spec)
        out_shape.append(jax.ShapeDtypeStruct((t, D_MODEL), BF16))
    outs = pl.pallas_call(
        functools.partial(_proj_kernel, epilogue=epilogue, h_mode=h_mode,
                          side_paired=tuple(paired for _, _, paired in side)),
        grid=grid,
        in_specs=in_specs + s_in,
        out_specs=out_specs + s_out,
        out_shape=out_shape + s_shapes,
        scratch_shapes=[pltpu.VMEM((PROJ_TM, D_MODEL), BF16)] if h_mode == "scratch" else [],
        compiler_params=_params(("parallel", "arbitrary")),
        name="proj_" + epilogue + "_h" + h_mode,
    )(*args, *s_args)
    n_main = len(out_specs)
    return outs[0], (outs[1] if h_mode == "emit" else None), list(outs[n_main:])


def _out_kernel(a_ref, w_ref, x_ref, *rest, side_paired):
    n_side = len(side_paired)
    side_srcs, o_ref, side_dsts = rest[:n_side], rest[n_side], rest[n_side + 1:]
    _side_cast(side_srcs, side_dsts, side_paired)
    o_ref[...] = x_ref[...] + jnp.dot(a_ref[...], w_ref[...], preferred_element_type=F32)


def _out_proj(a, w, layer, x, side=()):
    t, k = a.shape
    steps = t // OUT_TM
    assert not side or steps >= SIDE_CHUNKS
    s_in, s_out, s_shapes, s_args = _side_cast_specs(side, lambda i: i)
    outs = pl.pallas_call(
        functools.partial(_out_kernel, side_paired=tuple(paired for _, _, paired in side)),
        grid=(steps,),
        in_specs=[
            pl.BlockSpec((OUT_TM, k), lambda i: (i, 0)),
            pl.BlockSpec((None, k, D_MODEL), lambda i: (layer, 0, 0), pipeline_mode=pl.Buffered(1)),
            pl.BlockSpec((OUT_TM, D_MODEL), lambda i: (i, 0)),
        ] + s_in,
        out_specs=[pl.BlockSpec((OUT_TM, D_MODEL), lambda i: (i, 0))] + s_out,
        out_shape=[jax.ShapeDtypeStruct((t, D_MODEL), F32)] + s_shapes,
        compiler_params=_params(("parallel",)),
        name="out_proj_residual",
    )(a, w, x, *s_args)
    return outs[0], list(outs[1:])


SCAN_NBLK = SEQ // SCAN_TC
SCAN_CHUNKS = SCAN_TC // SCAN_CHUNK


def _scan_kernel(dec_ref, q_ref, k_ref, v_ref, gf_ref, gb_ref, tab_ref, o_ref, r_ref, ob_ref):
    h = pl.program_id(1)
    t = pl.program_id(2)
    c_len = SCAN_CHUNK

    @pl.when((t == 0) | (t == SCAN_NBLK))
    def _():
        r_ref[...] = jnp.zeros_like(r_ref)

    def run(backward):
        inner = tab_ref[0]
        cross = jnp.concatenate([tab_ref[1]] * (RET_DV // c_len), axis=1)
        into = jnp.concatenate([tab_ref[2]] * (RET_DK // c_len), axis=1)
        dec = dec_ref[1 if backward else 0, h]
        gate_ref = gb_ref if backward else gf_ref
        blk = (SCAN_NBLK - 1 - t) if backward else (t - SCAN_NBLK)
        order = range(SCAN_CHUNKS - 1, -1, -1) if backward else range(SCAN_CHUNKS)
        for c in order:
            rows = pl.ds(c * c_len, c_len)
            seq_rows = pl.ds(pl.multiple_of(blk * SCAN_TC + c * c_len, c_len), c_len)
            q = q_ref[rows, :]
            k = k_ref[rows, :]
            v = v_ref[rows, :]
            s = lax.dot_general(q, k, (((1,), (1,)), ((), ())), preferred_element_type=F32) * inner
            kd = (k.astype(F32) * into).astype(BF16)
            r = r_ref[...]
            sv = jnp.dot(jnp.concatenate([s.astype(BF16), kd.T], axis=0), v, preferred_element_type=F32)
            o = sv[:c_len] + jnp.dot(q, r.astype(BF16), preferred_element_type=F32) * cross
            r_ref[...] = r * dec + sv[c_len:]
            on = o * lax.rsqrt(jnp.mean(o * o, axis=-1, keepdims=True) + RMS_EPS)
            m = gate_ref[rows, :].astype(F32) * on
            if backward:
                ob_ref[seq_rows, :] = m
            else:
                o_ref[rows, :] = (m + ob_ref[seq_rows, :]).astype(BF16)

    @pl.when(t < SCAN_NBLK)
    def _():
        o_ref[...] = jnp.zeros_like(o_ref)
        run(True)

    @pl.when(t >= SCAN_NBLK)
    def _():
        run(False)


def _retention_scan(qk, v, gates, tabs, dec):
    t = qk.shape[0]
    batch = t // SEQ
    nb = SCAN_NBLK

    def bwd_blk(s):
        return jnp.where(s < nb, nb - 1 - s, 0)

    def fwd_blk(s):
        return jnp.where(s < nb, 0, s - nb)

    def cur_blk(s):
        return jnp.where(s < nb, nb - 1 - s, s - nb)

    nh = RET_HEADS
    in_specs = [
        pl.BlockSpec(memory_space=pltpu.SMEM),
        pl.BlockSpec((SCAN_TC, RET_DK), lambda b, h, s: (b * nb + cur_blk(s), h)),
        pl.BlockSpec((SCAN_TC, RET_DK), lambda b, h, s: (b * nb + cur_blk(s), nh + h)),
        pl.BlockSpec((SCAN_TC, RET_DV), lambda b, h, s: (b * nb + cur_blk(s), h)),
        pl.BlockSpec((SCAN_TC, RET_DV), lambda b, h, s: (b * nb + fwd_blk(s), h)),
        pl.BlockSpec((SCAN_TC, RET_DV), lambda b, h, s: (b * nb + bwd_blk(s), nh + h)),
        pl.BlockSpec((None, None, 3, SCAN_CHUNK, SCAN_CHUNK),
                     lambda b, h, s: (jnp.where(s < nb, 1, 0), h, 0, 0, 0)),
    ]
    return pl.pallas_call(
        _scan_kernel,
        grid=(batch, RET_HEADS, 2 * nb),
        in_specs=in_specs,
        out_specs=pl.BlockSpec((SCAN_TC, RET_DV), lambda b, h, s: (b * nb + fwd_blk(s), h)),
        out_shape=jax.ShapeDtypeStruct((t, RET_VDIM), BF16),
        scratch_shapes=[pltpu.VMEM((RET_DK, RET_DV), F32), pltpu.VMEM((SEQ, RET_DV), F32)],
        compiler_params=_params(("arbitrary", "arbitrary", "arbitrary")),
        name="retention_scan",
    )(dec, qk, qk, v, gates, gates, tabs)


def _retention_tables(decay_f, decay_b):
    c = SCAN_CHUNK
    pos = jnp.arange(c, dtype=F32)
    ones = jnp.ones((1, 1, c), F32)

    def tables(log_gamma, backward):
        lg = log_gamma[:, None, None]
        diff = pos[:, None] - pos[None, :]
        if backward:
            diff = -diff
        inner = jnp.where(diff[None] >= 0, jnp.exp(lg * jnp.maximum(diff, 0.0)[None]), 0.0)
        cross_pos = (c - pos) if backward else (pos + 1.0)
        into_pos = pos if backward else (c - 1.0 - pos)
        cross = jnp.exp(log_gamma[:, None] * cross_pos[None])[:, :, None] * ones
        into = jnp.exp(log_gamma[:, None] * into_pos[None])[:, :, None] * ones
        return jnp.stack([inner, cross, into], axis=1), jnp.exp(log_gamma * c)

    lg_f = jnp.log1p(-jnp.exp(decay_f.astype(F32)))
    lg_b = jnp.log1p(-jnp.exp(decay_b.astype(F32)))
    tab_f, dec_f = tables(lg_f, False)
    tab_b, dec_b = tables(lg_b, True)
    return jnp.stack([tab_f, tab_b]), jnp.stack([dec_f, dec_b])


def _rotary_tables():
    d = RET_DK
    inv = ROPE_BASE ** (-jnp.arange(0, d, 2, dtype=F32) / d)
    ang = jnp.arange(SEQ, dtype=F32)[:, None] * inv[None, :]
    return jnp.cos(ang), jnp.sin(ang)


NA_QROWS = 4
NA_GROUPS = GRID_ROWS // NA_QROWS
NA_SLAB_ROWS = NA_QROWS + NA_KH
NA_QTOK = NA_QROWS * GRID_W
NA_SLAB = NA_SLAB_ROWS * GRID_W


NA_GEOMETRIES = (
    lambda rq, rk: (rk < NA_KH, rk - rq + NA_KH - 1),
    lambda rq, rk: (rq <= rk < rq + NA_KH, rk - rq + NA_KH // 2 - 1),
    lambda rq, rk: (rk >= NA_QROWS, rk - rq - 1),
)


def _na_kernel(q_ref, k_ref, v_ref, toe_ref, o_ref, s_ref, bias_ref):
    scale = NA_HEAD_DIM ** -0.5 * LOG2_E

    @pl.when(pl.program_id(1) == 0)
    def _():
        first_row = lax.broadcasted_iota(jnp.int32, (GRID_W, 2 * GRID_W), 1) < GRID_W
        masked = jnp.full((GRID_W, 2 * GRID_W), MASK_VALUE, F32)
        for kind, geometry in enumerate(NA_GEOMETRIES):
            for rq in range(NA_QROWS):
                for rk in range(0, NA_SLAB_ROWS, 2):
                    (ok0, dr0), (ok1, dr1) = geometry(rq, rk), geometry(rq, rk + 1)
                    tile = jnp.where(first_row, toe_ref[dr0] if ok0 else masked,
                                     toe_ref[dr1] if ok1 else masked) if (ok0 or ok1) else masked
                    bias_ref[kind, rq * GRID_W:(rq + 1) * GRID_W, rk * GRID_W:(rk + 2) * GRID_W] = tile

    def windows(g):
        s0 = jnp.clip(NA_QROWS * g - NA_KH // 2, 0, GRID_ROWS - NA_SLAB_ROWS)
        rows = pl.ds(pl.multiple_of(g * NA_QTOK, NA_QTOK), NA_QTOK)
        slab = pl.ds(pl.multiple_of(s0 * GRID_W, NA_QTOK), NA_SLAB)
        return rows, slab

    def scores(g, slot):
        rows, slab = windows(g)
        kind = jnp.where(g == 0, 0, jnp.where(g == NA_GROUPS - 1, 2, 1))
        s = lax.dot_general(q_ref[rows, :], k_ref[slab, :], (((1,), (1,)), ((), ())),
                            preferred_element_type=F32)
        s_ref[slot] = s * scale + bias_ref[kind]

    def attend(g, slot):
        rows, slab = windows(g)
        s = s_ref[slot]
        e = jnp.exp2(s - jnp.max(s, axis=-1, keepdims=True))
        o = jnp.dot(e.astype(BF16), v_ref[slab, :], preferred_element_type=F32)
        o_ref[rows, :] = (o * (1.0 / jnp.sum(e, axis=-1, keepdims=True))).astype(BF16)

    scores(0, 0)

    def pair(i, carry):
        g = 2 * i
        scores(g + 1, 1)
        attend(g, 0)
        scores(g + 2, 0)
        attend(g + 1, 1)
        return carry

    lax.fori_loop(0, NA_GROUPS // 2 - 1, pair, 0, unroll=True)
    scores(NA_GROUPS - 1, 1)
    attend(NA_GROUPS - 2, 0)
    attend(NA_GROUPS - 1, 1)


def _na_bias_tables(rpb):
    pad = GRID_W - NA_KW
    padded = jnp.pad(rpb.astype(F32), ((0, 0), (0, 0), (pad, pad)))
    toe = jnp.stack([padded[:, :, GRID_W - 1 - c: 2 * GRID_W - 1 - c] for c in range(GRID_W)], axis=2)
    cq = np.arange(GRID_W)
    ck = np.arange(GRID_W)
    qwin = np.clip(cq - NA_KW // 2, 0, GRID_W - NA_KW)
    col_ok = (ck[None, :] >= qwin[:, None]) & (ck[None, :] < qwin[:, None] + NA_KW)
    toe = jnp.where(col_ok[None, None], toe * LOG2_E, MASK_VALUE)
    return jnp.concatenate([toe, toe], axis=-1)


def _neighbourhood_attention(qkv, bias):
    t = qkv.shape[0]
    batch = t // SEQ
    hd = NA_HEAD_DIM
    return pl.pallas_call(
        _na_kernel,
        grid=(NA_HEADS, batch),
        in_specs=[
            pl.BlockSpec((SEQ, hd), lambda h, b: (b, h)),
            pl.BlockSpec((SEQ, hd), lambda h, b: (b, NA_HEADS + h)),
            pl.BlockSpec((SEQ, hd), lambda h, b: (b, 2 * NA_HEADS + h)),
            pl.BlockSpec((None, 2 * NA_KH - 1, GRID_W, 2 * GRID_W), lambda h, b: (h, 0, 0, 0)),
        ],
        out_specs=pl.BlockSpec((SEQ, hd), lambda h, b: (b, h)),
        out_shape=jax.ShapeDtypeStruct((t, D_MODEL), BF16),
        scratch_shapes=[pltpu.VMEM((2, NA_QTOK, NA_SLAB), F32), pltpu.VMEM((3, NA_QTOK, NA_SLAB), F32)],
        compiler_params=_params(("arbitrary", "arbitrary")),
        name="neighbourhood_attention",
    )(qkv, qkv, qkv, bias)


def _trunk(x, weights, tables, raw):
    assert DEPTH == 2

    def side(names):
        return [(raw[name], layer, name in ("ffn1_w_in", "ffn2_w_in")) for name, layer in names] if raw else []

    def publish(names, casted):
        for (name, layer), w in zip(names, casted):
            weights[name, layer] = (w, 0)

    for i in range(DEPTH):
        x = _ffn(x, tables["norm_ffn1"][i], *weights["ffn1_w_in", i], *weights["ffn1_w_out", i])
        j = i // 2
        if i % 2 == 0:
            w_in, lj = weights["ret_w_in", j]
            qk, h, _ = _proj(x, tables["norm_mix"][i], w_in, lj, (0, 2 * D_MODEL), rotary=tables["rotary"],
                             emit_h=True)
            names = [("ffn2_w_in", i), ("ffn2_w_out", i), ("na_w_in", j)]
            v, _, casted = _proj(h, None, w_in, lj, (2 * D_MODEL, 2 * D_MODEL + RET_VDIM), side=side(names))
            publish(names, casted)
            names = [("ffn1_w_in", i + 1), ("ffn1_w_out", i + 1), ("na_w_out", j), ("ret_w_out", j)]
            gates, _, casted = _proj(h, None, w_in, lj, (2 * D_MODEL + RET_VDIM, RET_IN),
                                     gate_gain=jnp.concatenate([tables["ret_gn_f"][j], tables["ret_gn_b"][j]]),
                                     side=side(names))
            publish(names, casted)
            tabs, dec = tables["ret_tables"][j]
            mixed = _retention_scan(qk, v, gates, tabs, dec)
            names = [("ffn2_w_in", i + 1)]
            x, casted = _out_proj(mixed, *weights["ret_w_out", j], x, side=side(names))
            publish(names, casted)
        else:
            names = [("ffn2_w_out", i)]
            qkv, _, casted = _proj(x, tables["norm_mix"][i], *weights["na_w_in", j], (0, 3 * D_MODEL),
                                   side=side(names))
            publish(names, casted)
            att = _neighbourhood_attention(qkv, tables["na_bias"][j])
            x, _ = _out_proj(att, *weights["na_w_out", j], x)
        x = _ffn(x, tables["norm_ffn2"][i], *weights["ffn2_w_in", i], *weights["ffn2_w_out", i],
                 final_gain=tables["norm_final"] if i == DEPTH - 1 else None)
    return x


def kernel(x_prompt, x_sample, norm_ffn1, ffn1_w_in, ffn1_w_out, norm_mix, norm_ffn2, ffn2_w_in, ffn2_w_out,
           ret_w_in, ret_w_out, ret_decay_f, ret_decay_b, ret_gn_f, ret_gn_b, na_w_in, na_w_out, na_rpb,
           norm_final):
    n_ret = ret_w_in.shape[0]
    n_na = na_w_in.shape[0]
    tables = dict(
        norm_ffn1=norm_ffn1, norm_mix=norm_mix, norm_ffn2=norm_ffn2, norm_final=norm_final,
        ret_gn_f=ret_gn_f, ret_gn_b=ret_gn_b,
        ret_tables=[_retention_tables(ret_decay_f[j], ret_decay_b[j]) for j in range(n_ret)],
        rotary=_rotary_tables(),
        na_bias=[_na_bias_tables(na_rpb[j]) for j in range(n_na)],
    )
    weights = {}
    up_front = dict(ffn1_w_in=_pair_gate_up(ffn1_w_in[:1]), ffn1_w_out=ffn1_w_out[:1], ret_w_in=ret_w_in)
    for name, w in up_front.items():
        stack = w.astype(BF16)
        for layer in range(w.shape[0]):
            weights[name, layer] = (stack, layer)
    raw = dict(ffn1_w_in=ffn1_w_in, ffn1_w_out=ffn1_w_out, ffn2_w_in=ffn2_w_in, ffn2_w_out=ffn2_w_out,
               ret_w_out=ret_w_out, na_w_in=na_w_in, na_w_out=na_w_out)
    outs = []
    for x in (x_prompt, x_sample):
        y = _trunk(x.reshape(-1, D_MODEL), weights, tables, raw)
        raw = None
        outs.append(y.reshape(x.shape))
    return tuple(outs)
```

```python
import functools

import numpy as np
import jax
import jax.numpy as jnp
from jax import lax
from jax.experimental import pallas as pl
from jax.experimental.pallas import tpu as pltpu

D_MODEL = 2048
SEQ = 4096
DEPTH = 2
GRID_W = 64
D_FF = 5632
RET_HEADS = 8
RET_DK = D_MODEL // RET_HEADS
RET_DV = 2 * RET_DK
RET_VDIM = RET_HEADS * RET_DV
RET_IN = 2 * D_MODEL + 3 * RET_VDIM
ROPE_BASE = 10000.0
NA_HEADS = 16
NA_HEAD_DIM = D_MODEL // NA_HEADS
NA_KH = 8
NA_KW = 16
RMS_EPS = 1e-6
MASK_VALUE = -1e30
LOG2_E = 1.4426950408889634

F32 = jnp.float32
BF16 = jnp.bfloat16

V7X_VMEM_BYTES = 64 * 1024 * 1024
VMEM_LIMIT_BYTES = V7X_VMEM_BYTES - 8 * 1024 * 1024

FFN_TM, FFN_TF = 1024, 512
PROJ_TM, PROJ_TN = 1024, 2048
OUT_TM = 512
SCAN_TC = 4096
SCAN_CHUNK = 256
GRID_ROWS = SEQ // GRID_W


def _params(semantics):
    return pltpu.CompilerParams(dimension_semantics=semantics, vmem_limit_bytes=VMEM_LIMIT_BYTES)


def _rmsnorm(xf, g):
    ms = jnp.mean(xf * xf, axis=-1, keepdims=True)
    return xf * lax.rsqrt(ms + RMS_EPS) * g


def _silu(g):
    return g / (1.0 + jnp.exp(-g))


def _ffn_kernel(x_ref, g_ref, wg_ref, wu_ref, wo_ref, *rest, final_norm):
    if final_norm:
        gfin_ref, o_ref, h_ref = rest
    else:
        o_ref, h_ref = rest
    j = pl.program_id(1)

    @pl.when(j == 0)
    def _():
        x = x_ref[...]
        h_ref[...] = _rmsnorm(x, g_ref[...]).astype(BF16)
        o_ref[...] = x

    h = h_ref[...]
    g = jnp.dot(h, wg_ref[...], preferred_element_type=F32)
    u = jnp.dot(h, wu_ref[...], preferred_element_type=F32)
    a = (_silu(g) * u * 0.5).astype(BF16)
    o_ref[...] += jnp.dot(a, wo_ref[...], preferred_element_type=F32)

    if final_norm:
        @pl.when(j == pl.num_programs(1) - 1)
        def _():
            o_ref[...] = _rmsnorm(o_ref[...], gfin_ref[...])


def _ffn(x, gain, w_in, layer_in, w_out, layer_out, final_gain=None):
    t = x.shape[0]
    nj = D_FF // FFN_TF
    in_specs = [
        pl.BlockSpec((FFN_TM, D_MODEL), lambda i, j: (i, 0)),
        pl.BlockSpec((1, D_MODEL), lambda i, j: (0, 0)),
        pl.BlockSpec((None, D_MODEL, FFN_TF), lambda i, j: (layer_in, 0, j)),
        pl.BlockSpec((None, D_MODEL, FFN_TF), lambda i, j: (layer_in, 0, nj + j)),
        pl.BlockSpec((None, FFN_TF, D_MODEL), lambda i, j: (layer_out, j, 0)),
    ]
    args = [x, gain.reshape(1, D_MODEL), w_in, w_in, w_out]
    if final_gain is not None:
        in_specs.append(pl.BlockSpec((1, D_MODEL), lambda i, j: (0, 0)))
        args.append(final_gain.reshape(1, D_MODEL))
    return pl.pallas_call(
        functools.partial(_ffn_kernel, final_norm=final_gain is not None),
        grid=(t // FFN_TM, nj),
        in_specs=in_specs,
        out_specs=pl.BlockSpec((FFN_TM, D_MODEL), lambda i, j: (i, 0)),
        out_shape=jax.ShapeDtypeStruct((t, D_MODEL), F32),
        scratch_shapes=[pltpu.VMEM((FFN_TM, D_MODEL), BF16)],
        compiler_params=_params(("parallel", "arbitrary")),
        name="swiglu_ffn",
    )(*args)


SIDE_CHUNKS = 32


def _side_cast_specs(jobs, step_of):
    in_specs, out_specs, out_shapes, args = [], [], [], []
    for w, layer in jobs:
        _, r, c = w.shape
        rows = r // SIDE_CHUNKS
        assert rows * SIDE_CHUNKS == r and rows % 16 == 0

        def chunk(*g):
            return jnp.minimum(step_of(*g), SIDE_CHUNKS - 1)

        in_specs.append(pl.BlockSpec((None, rows, c), lambda *g, layer=layer: (layer, chunk(*g), 0)))
        out_specs.append(pl.BlockSpec((None, rows, c), lambda *g: (0, chunk(*g), 0)))
        out_shapes.append(jax.ShapeDtypeStruct((1, r, c), BF16))
        args.append(w)
    return in_specs, out_specs, out_shapes, args


def _side_cast(srcs, dsts):
    for src, dst in zip(srcs, dsts):
        dst[...] = src[...].astype(BF16)


def _proj_kernel(*refs, epilogue, h_mode, n_side):
    refs = list(refs)
    if h_mode == "input":
        h_ref = refs.pop(0)
    else:
        x_ref, g_ref = refs.pop(0), refs.pop(0)
    w_ref = refs.pop(0)
    if epilogue == "rotary":
        cos_ref, sin_ref = refs.pop(0), refs.pop(0)
    elif epilogue == "gate":
        gn_ref = refs.pop(0)
    side_srcs = [refs.pop(0) for _ in range(n_side)]
    o_ref = refs.pop(0)
    if h_mode == "emit":
        h_ref = refs.pop(0)
    side_dsts = [refs.pop(0) for _ in range(n_side)]
    if h_mode == "scratch":
        h_ref = refs.pop(0)
    n = pl.program_id(1)
    _side_cast(side_srcs, side_dsts)

    if h_mode != "input":
        @pl.when(n == 0)
        def _():
            h_ref[...] = _rmsnorm(x_ref[...], g_ref[...]).astype(BF16)

    y = jnp.dot(h_ref[...], w_ref[...], preferred_element_type=F32)
    if epilogue == "plain":
        o_ref[...] = y.astype(BF16)
        return
    if epilogue == "gate":
        o_ref[...] = (_silu(y) * gn_ref[...]).astype(BF16)
        return
    cos = cos_ref[...]
    sin = sin_ref[...]
    scale = jnp.where(n < pl.num_programs(1) // 2, 1.0, RET_DK ** -0.5).astype(F32)
    half = RET_DK // 2
    for hh in range(PROJ_TN // RET_DK):
        x1 = y[:, hh * RET_DK: hh * RET_DK + half]
        x2 = y[:, hh * RET_DK + half: (hh + 1) * RET_DK]
        o_ref[:, hh * RET_DK: hh * RET_DK + half] = ((x1 * cos - x2 * sin) * scale).astype(BF16)
        o_ref[:, hh * RET_DK + half: (hh + 1) * RET_DK] = ((x1 * sin + x2 * cos) * scale).astype(BF16)


def _proj(x, gain, w, layer, cols, rotary=None, gate_gain=None, emit_h=False, side=()):
    t = x.shape[0]
    n_out = cols[1] - cols[0]
    col0 = cols[0] // PROJ_TN
    grid = (t // PROJ_TM, n_out // PROJ_TN)
    row_spec = pl.BlockSpec((PROJ_TM, D_MODEL), lambda i, n: (i, 0))
    h_mode = "input" if gain is None else ("emit" if emit_h else "scratch")
    in_specs = [row_spec]
    args = [x]
    if gain is not None:
        in_specs.append(pl.BlockSpec((1, D_MODEL), lambda i, n: (0, 0)))
        args.append(gain.reshape(1, D_MODEL))
    in_specs.append(pl.BlockSpec((None, D_MODEL, PROJ_TN), lambda i, n: (layer, 0, col0 + n)))
    args.append(w)
    epilogue = "plain"
    if rotary is not None:
        epilogue = "rotary"
        seq_tiles = SEQ // PROJ_TM
        tab_spec = pl.BlockSpec((PROJ_TM, RET_DK // 2), lambda i, n: (i % seq_tiles, 0))
        in_specs += [tab_spec, tab_spec]
        args += list(rotary)
    elif gate_gain is not None:
        epilogue = "gate"
        in_specs.append(pl.BlockSpec((1, PROJ_TN), lambda i, n: (0, n)))
        args.append(gate_gain.reshape(1, n_out).astype(F32))
    assert not side or grid[0] * grid[1] >= SIDE_CHUNKS
    s_in, s_out, s_shapes, s_args = _side_cast_specs(side, lambda i, n: i * grid[1] + n)
    out_specs = [pl.BlockSpec((PROJ_TM, PROJ_TN), lambda i, n: (i, n))]
    out_shape = [jax.ShapeDtypeStruct((t, n_out), BF16)]
    if h_mode == "emit":
        out_specs.append(row_spec)
        out_shape.append(jax.ShapeDtypeStruct((t, D_MODEL), BF16))
    outs = pl.pallas_call(
        functools.partial(_proj_kernel, epilogue=epilogue, h_mode=h_mode, n_side=len(side)),
        grid=grid,
        in_specs=in_specs + s_in,
        out_specs=out_specs + s_out,
        out_shape=out_shape + s_shapes,
        scratch_shapes=[pltpu.VMEM((PROJ_TM, D_MODEL), BF16)] if h_mode == "scratch" else [],
        compiler_params=_params(("parallel", "arbitrary")),
        name="proj_" + epilogue + "_h" + h_mode,
    )(*args, *s_args)
    n_main = len(out_specs)
    return outs[0], (outs[1] if h_mode == "emit" else None), list(outs[n_main:])


def _out_kernel(a_ref, w_ref, x_ref, *rest, n_side):
    side_srcs, o_ref, side_dsts = rest[:n_side], rest[n_side], rest[n_side + 1:]
    _side_cast(side_srcs, side_dsts)
    o_ref[...] = x_ref[...] + jnp.dot(a_ref[...], w_ref[...], preferred_element_type=F32)


def _out_proj(a, w, layer, x, side=()):
    t, k = a.shape
    steps = t // OUT_TM
    assert not side or steps >= SIDE_CHUNKS
    s_in, s_out, s_shapes, s_args = _side_cast_specs(side, lambda i: i)
    outs = pl.pallas_call(
        functools.partial(_out_kernel, n_side=len(side)),
        grid=(steps,),
        in_specs=[
            pl.BlockSpec((OUT_TM, k), lambda i: (i, 0)),
            pl.BlockSpec((None, k, D_MODEL), lambda i: (layer, 0, 0), pipeline_mode=pl.Buffered(1)),
            pl.BlockSpec((OUT_TM, D_MODEL), lambda i: (i, 0)),
        ] + s_in,
        out_specs=[pl.BlockSpec((OUT_TM, D_MODEL), lambda i: (i, 0))] + s_out,
        out_shape=[jax.ShapeDtypeStruct((t, D_MODEL), F32)] + s_shapes,
        compiler_params=_params(("parallel",)),
        name="out_proj_residual",
    )(a, w, x, *s_args)
    return outs[0], list(outs[1:])


SCAN_NBLK = SEQ // SCAN_TC
SCAN_CHUNKS = SCAN_TC // SCAN_CHUNK


def _scan_kernel(dec_ref, q_ref, k_ref, v_ref, gf_ref, gb_ref, tab_ref, o_ref, r_ref, ob_ref):
    h = pl.program_id(1)
    t = pl.program_id(2)
    c_len = SCAN_CHUNK

    @pl.when((t == 0) | (t == SCAN_NBLK))
    def _():
        r_ref[...] = jnp.zeros_like(r_ref)

    def run(backward):
        inner = tab_ref[0]
        cross = jnp.concatenate([tab_ref[1]] * (RET_DV // c_len), axis=1)
        into = jnp.concatenate([tab_ref[2]] * (RET_DK // c_len), axis=1)
        dec = dec_ref[1 if backward else 0, h]
        gate_ref = gb_ref if backward else gf_ref
        blk = (SCAN_NBLK - 1 - t) if backward else (t - SCAN_NBLK)
        order = range(SCAN_CHUNKS - 1, -1, -1) if backward else range(SCAN_CHUNKS)
        for c in order:
            rows = pl.ds(c * c_len, c_len)
            seq_rows = pl.ds(pl.multiple_of(blk * SCAN_TC + c * c_len, c_len), c_len)
            q = q_ref[rows, :]
            k = k_ref[rows, :]
            v = v_ref[rows, :]
            s = lax.dot_general(q, k, (((1,), (1,)), ((), ())), preferred_element_type=F32) * inner
            kd = (k.astype(F32) * into).astype(BF16)
            r = r_ref[...]
            sv = jnp.dot(jnp.concatenate([s.astype(BF16), kd.T], axis=0), v, preferred_element_type=F32)
            o = sv[:c_len] + jnp.dot(q, r.astype(BF16), preferred_element_type=F32) * cross
            r_ref[...] = r * dec + sv[c_len:]
            on = o * lax.rsqrt(jnp.mean(o * o, axis=-1, keepdims=True) + RMS_EPS)
            m = gate_ref[rows, :].astype(F32) * on
            if backward:
                ob_ref[seq_rows, :] = m
            else:
                o_ref[rows, :] = (m + ob_ref[seq_rows, :]).astype(BF16)

    @pl.when(t < SCAN_NBLK)
    def _():
        o_ref[...] = jnp.zeros_like(o_ref)
        run(True)

    @pl.when(t >= SCAN_NBLK)
    def _():
        run(False)


def _retention_scan(qk, v, gates, tabs, dec):
    t = qk.shape[0]
    batch = t // SEQ
    nb = SCAN_NBLK

    def bwd_blk(s):
        return jnp.where(s < nb, nb - 1 - s, 0)

    def fwd_blk(s):
        return jnp.where(s < nb, 0, s - nb)

    def cur_blk(s):
        return jnp.where(s < nb, nb - 1 - s, s - nb)

    nh = RET_HEADS
    in_specs = [
        pl.BlockSpec(memory_space=pltpu.SMEM),
        pl.BlockSpec((SCAN_TC, RET_DK), lambda b, h, s: (b * nb + cur_blk(s), h)),
        pl.BlockSpec((SCAN_TC, RET_DK), lambda b, h, s: (b * nb + cur_blk(s), nh + h)),
        pl.BlockSpec((SCAN_TC, RET_DV), lambda b, h, s: (b * nb + cur_blk(s), h)),
        pl.BlockSpec((SCAN_TC, RET_DV), lambda b, h, s: (b * nb + fwd_blk(s), h)),
        pl.BlockSpec((SCAN_TC, RET_DV), lambda b, h, s: (b * nb + bwd_blk(s), nh + h)),
        pl.BlockSpec((None, None, 3, SCAN_CHUNK, SCAN_CHUNK),
                     lambda b, h, s: (jnp.where(s < nb, 1, 0), h, 0, 0, 0)),
    ]
    return pl.pallas_call(
        _scan_kernel,
        grid=(batch, RET_HEADS, 2 * nb),
        in_specs=in_specs,
        out_specs=pl.BlockSpec((SCAN_TC, RET_DV), lambda b, h, s: (b * nb + fwd_blk(s), h)),
        out_shape=jax.ShapeDtypeStruct((t, RET_VDIM), BF16),
        scratch_shapes=[pltpu.VMEM((RET_DK, RET_DV), F32), pltpu.VMEM((SEQ, RET_DV), F32)],
        compiler_params=_params(("arbitrary", "arbitrary", "arbitrary")),
        name="retention_scan",
    )(dec, qk, qk, v, gates, gates, tabs)


def _retention_tables(decay_f, decay_b):
    c = SCAN_CHUNK
    pos = jnp.arange(c, dtype=F32)
    ones = jnp.ones((1, 1, c), F32)

    def tables(log_gamma, backward):
        lg = log_gamma[:, None, None]
        diff = pos[:, None] - pos[None, :]
        if backward:
            diff = -diff
        inner = jnp.where(diff[None] >= 0, jnp.exp(lg * jnp.maximum(diff, 0.0)[None]), 0.0)
        cross_pos = (c - pos) if backward else (pos + 1.0)
        into_pos = pos if backward else (c - 1.0 - pos)
        cross = jnp.exp(log_gamma[:, None] * cross_pos[None])[:, :, None] * ones
        into = jnp.exp(log_gamma[:, None] * into_pos[None])[:, :, None] * ones
        return jnp.stack([inner, cross, into], axis=1), jnp.exp(log_gamma * c)

    lg_f = jnp.log1p(-jnp.exp(decay_f.astype(F32)))
    lg_b = jnp.log1p(-jnp.exp(decay_b.astype(F32)))
    tab_f, dec_f = tables(lg_f, False)
    tab_b, dec_b = tables(lg_b, True)
    return jnp.stack([tab_f, tab_b]), jnp.stack([dec_f, dec_b])


def _rotary_tables():
    d = RET_DK
    inv = ROPE_BASE ** (-jnp.arange(0, d, 2, dtype=F32) / d)
    ang = jnp.arange(SEQ, dtype=F32)[:, None] * inv[None, :]
    return jnp.cos(ang), jnp.sin(ang)


NA_QROWS = 4
NA_GROUPS = GRID_ROWS // NA_QROWS
NA_SLAB_ROWS = NA_QROWS + NA_KH
NA_QTOK = NA_QROWS * GRID_W
NA_SLAB = NA_SLAB_ROWS * GRID_W


NA_GEOMETRIES = (
    lambda rq, rk: (rk < NA_KH, rk - rq + NA_KH - 1),
    lambda rq, rk: (rq <= rk < rq + NA_KH, rk - rq + NA_KH // 2 - 1),
    lambda rq, rk: (rk >= NA_QROWS, rk - rq - 1),
)


def _na_kernel(q_ref, k_ref, v_ref, toe_ref, o_ref, s_ref, bias_ref):
    scale = NA_HEAD_DIM ** -0.5 * LOG2_E

    @pl.when(pl.program_id(1) == 0)
    def _():
        first_row = lax.broadcasted_iota(jnp.int32, (GRID_W, 2 * GRID_W), 1) < GRID_W
        masked = jnp.full((GRID_W, 2 * GRID_W), MASK_VALUE, F32)
        for kind, geometry in enumerate(NA_GEOMETRIES):
            for rq in range(NA_QROWS):
                for rk in range(0, NA_SLAB_ROWS, 2):
                    (ok0, dr0), (ok1, dr1) = geometry(rq, rk), geometry(rq, rk + 1)
                    tile = jnp.where(first_row, toe_ref[dr0] if ok0 else masked,
                                     toe_ref[dr1] if ok1 else masked) if (ok0 or ok1) else masked
                    bias_ref[kind, rq * GRID_W:(rq + 1) * GRID_W, rk * GRID_W:(rk + 2) * GRID_W] = tile

    def windows(g):
        s0 = jnp.clip(NA_QROWS * g - NA_KH // 2, 0, GRID_ROWS - NA_SLAB_ROWS)
        rows = pl.ds(pl.multiple_of(g * NA_QTOK, NA_QTOK), NA_QTOK)
        slab = pl.ds(pl.multiple_of(s0 * GRID_W, NA_QTOK), NA_SLAB)
        return rows, slab

    def scores(g, slot):
        rows, slab = windows(g)
        kind = jnp.where(g == 0, 0, jnp.where(g == NA_GROUPS - 1, 2, 1))
        s = lax.dot_general(q_ref[rows, :], k_ref[slab, :], (((1,), (1,)), ((), ())),
                            preferred_element_type=F32)
        s_ref[slot] = s * scale + bias_ref[kind]

    def attend(g, slot):
        rows, slab = windows(g)
        s = s_ref[slot]
        e = jnp.exp2(s - jnp.max(s, axis=-1, keepdims=True))
        o = jnp.dot(e.astype(BF16), v_ref[slab, :], preferred_element_type=F32)
        o_ref[rows, :] = (o * (1.0 / jnp.sum(e, axis=-1, keepdims=True))).astype(BF16)

    scores(0, 0)

    def pair(i, carry):
        g = 2 * i
        scores(g + 1, 1)
        attend(g, 0)
        scores(g + 2, 0)
        attend(g + 1, 1)
        return carry

    lax.fori_loop(0, NA_GROUPS // 2 - 1, pair, 0, unroll=True)
    scores(NA_GROUPS - 1, 1)
    attend(NA_GROUPS - 2, 0)
    attend(NA_GROUPS - 1, 1)


def _na_bias_tables(rpb):
    pad = GRID_W - NA_KW
    padded = jnp.pad(rpb.astype(F32), ((0, 0), (0, 0), (pad, pad)))
    toe = jnp.stack([padded[:, :, GRID_W - 1 - c: 2 * GRID_W - 1 - c] for c in range(GRID_W)], axis=2)
    cq = np.arange(GRID_W)
    ck = np.arange(GRID_W)
    qwin = np.clip(cq - NA_KW // 2, 0, GRID_W - NA_KW)
    col_ok = (ck[None, :] >= qwin[:, None]) & (ck[None, :] < qwin[:, None] + NA_KW)
    toe = jnp.where(col_ok[None, None], toe * LOG2_E, MASK_VALUE)
    return jnp.concatenate([toe, toe], axis=-1)


def _neighbourhood_attention(qkv, bias):
    t = qkv.shape[0]
    batch = t // SEQ
    hd = NA_HEAD_DIM
    return pl.pallas_call(
        _na_kernel,
        grid=(NA_HEADS, batch),
        in_specs=[
            pl.BlockSpec((SEQ, hd), lambda h, b: (b, h)),
            pl.BlockSpec((SEQ, hd), lambda h, b: (b, NA_HEADS + h)),
            pl.BlockSpec((SEQ, hd), lambda h, b: (b, 2 * NA_HEADS + h)),
            pl.BlockSpec((None, 2 * NA_KH - 1, GRID_W, 2 * GRID_W), lambda h, b: (h, 0, 0, 0)),
        ],
        out_specs=pl.BlockSpec((SEQ, hd), lambda h, b: (b, h)),
        out_shape=jax.ShapeDtypeStruct((t, D_MODEL), BF16),
        scratch_shapes=[pltpu.VMEM((2, NA_QTOK, NA_SLAB), F32), pltpu.VMEM((3, NA_QTOK, NA_SLAB), F32)],
        compiler_params=_params(("arbitrary", "arbitrary")),
        name="neighbourhood_attention",
    )(qkv, qkv, qkv, bias)


def _trunk(x, weights, tables, raw):
    assert DEPTH == 2

    def side(names):
        return [(raw[name], layer) for name, layer in names] if raw else []

    def publish(names, casted):
        for (name, layer), w in zip(names, casted):
            weights[name, layer] = (w, 0)

    for i in range(DEPTH):
        x = _ffn(x, tables["norm_ffn1"][i], *weights["ffn1_w_in", i], *weights["ffn1_w_out", i])
        j = i // 2
        if i % 2 == 0:
            w_in, lj = weights["ret_w_in", j]
            qk, h, _ = _proj(x, tables["norm_mix"][i], w_in, lj, (0, 2 * D_MODEL), rotary=tables["rotary"],
                             emit_h=True)
            names = [("ffn2_w_in", i), ("ffn2_w_out", i), ("na_w_in", j)]
            v, _, casted = _proj(h, None, w_in, lj, (2 * D_MODEL, 2 * D_MODEL + RET_VDIM), side=side(names))
            publish(names, casted)
            names = [("ffn1_w_in", i + 1), ("ffn1_w_out", i + 1), ("na_w_out", j), ("ret_w_out", j)]
            gates, _, casted = _proj(h, None, w_in, lj, (2 * D_MODEL + RET_VDIM, RET_IN),
                                     gate_gain=jnp.concatenate([tables["ret_gn_f"][j], tables["ret_gn_b"][j]]),
                                     side=side(names))
            publish(names, casted)
            tabs, dec = tables["ret_tables"][j]
            mixed = _retention_scan(qk, v, gates, tabs, dec)
            names = [("ffn2_w_in", i + 1)]
            x, casted = _out_proj(mixed, *weights["ret_w_out", j], x, side=side(names))
            publish(names, casted)
        else:
            names = [("ffn2_w_out", i)]
            qkv, _, casted = _proj(x, tables["norm_mix"][i], *weights["na_w_in", j], (0, 3 * D_MODEL),
                                   side=side(names))
            publish(names, casted)
            att = _neighbourhood_attention(qkv, tables["na_bias"][j])
            x, _ = _out_proj(att, *weights["na_w_out", j], x)
        x = _ffn(x, tables["norm_ffn2"][i], *weights["ffn2_w_in", i], *weights["ffn2_w_out", i],
                 final_gain=tables["norm_final"] if i == DEPTH - 1 else None)
    return x


def kernel(x_prompt, x_sample, norm_ffn1, ffn1_w_in, ffn1_w_out, norm_mix, norm_ffn2, ffn2_w_in, ffn2_w_out,
           ret_w_in, ret_w_out, ret_decay_f, ret_decay_b, ret_gn_f, ret_gn_b, na_w_in, na_w_out, na_rpb,
           norm_final):
    n_ret = ret_w_in.shape[0]
    n_na = na_w_in.shape[0]
    tables = dict(
        norm_ffn1=norm_ffn1, norm_mix=norm_mix, norm_ffn2=norm_ffn2, norm_final=norm_final,
        ret_gn_f=ret_gn_f, ret_gn_b=ret_gn_b,
        ret_tables=[_retention_tables(ret_decay_f[j], ret_decay_b[j]) for j in range(n_ret)],
        rotary=_rotary_tables(),
        na_bias=[_na_bias_tables(na_rpb[j]) for j in range(n_na)],
    )
    weights = {}
    up_front = dict(ffn1_w_in=ffn1_w_in[:1], ffn1_w_out=ffn1_w_out[:1], ret_w_in=ret_w_in)
    for name, w in up_front.items():
        stack = w.astype(BF16)
        for layer in range(w.shape[0]):
            weights[name, layer] = (stack, layer)
    raw = dict(ffn1_w_in=ffn1_w_in, ffn1_w_out=ffn1_w_out, ffn2_w_in=ffn2_w_in, ffn2_w_out=ffn2_w_out,
               ret_w_out=ret_w_out, na_w_in=na_w_in, na_w_out=na_w_out)
    outs = []
    for x in (x_prompt, x_sample):
        y = _trunk(x.reshape(-1, D_MODEL), weights, tables, raw)
        raw = None
        outs.append(y.reshape(x.shape))
    return tuple(outs)
```

```python
import functools

import numpy as np
import jax
import jax.numpy as jnp
from jax import lax
from jax.experimental import pallas as pl
from jax.experimental.pallas import tpu as pltpu

D_MODEL = 2048
SEQ = 4096
DEPTH = 2
GRID_W = 64
D_FF = 5632
RET_HEADS = 8
RET_DK = D_MODEL // RET_HEADS
RET_DV = 2 * RET_DK
RET_VDIM = RET_HEADS * RET_DV
RET_IN = 2 * D_MODEL + 3 * RET_VDIM
ROPE_BASE = 10000.0
NA_HEADS = 16
NA_HEAD_DIM = D_MODEL // NA_HEADS
NA_KH = 8
NA_KW = 16
RMS_EPS = 1e-6
MASK_VALUE = -1e30
LOG2_E = 1.4426950408889634
NA_Q_SCALE = NA_HEAD_DIM ** -0.5 * LOG2_E

F32 = jnp.float32
BF16 = jnp.bfloat16

V7X_VMEM_BYTES = 64 * 1024 * 1024
VMEM_LIMIT_BYTES = V7X_VMEM_BYTES - 8 * 1024 * 1024

FFN_TM, FFN_TF = 1024, 512
PROJ_TM, PROJ_TN = 1024, 2048
OUT_TM = 512
SCAN_TC = 2048
SCAN_CHUNK = 256
GRID_ROWS = SEQ // GRID_W


def _params(semantics):
    return pltpu.CompilerParams(dimension_semantics=semantics, vmem_limit_bytes=VMEM_LIMIT_BYTES)


def _rmsnorm(xf, g):
    ms = jnp.mean(xf * xf, axis=-1, keepdims=True)
    return xf * lax.rsqrt(ms + RMS_EPS) * g


def _silu(g):
    return g / (1.0 + jnp.exp(-g))


def _ffn_kernel(x_ref, g_ref, wg_ref, wu_ref, wo_ref, *rest, final_norm):
    if final_norm:
        gfin_ref, o_ref, h_ref = rest
    else:
        o_ref, h_ref = rest
    j = pl.program_id(1)

    @pl.when(j == 0)
    def _():
        x = x_ref[...]
        h_ref[...] = _rmsnorm(x, g_ref[...]).astype(BF16)
        o_ref[...] = x

    h = h_ref[...]
    g = jnp.dot(h, wg_ref[...], preferred_element_type=F32)
    u = jnp.dot(h, wu_ref[...], preferred_element_type=F32)
    a = (_silu(g) * u * 0.5).astype(BF16)
    o_ref[...] += jnp.dot(a, wo_ref[...], preferred_element_type=F32)

    if final_norm:
        @pl.when(j == pl.num_programs(1) - 1)
        def _():
            o_ref[...] = _rmsnorm(o_ref[...], gfin_ref[...])


def _ffn(x, gain, w_in, layer_in, w_out, layer_out, final_gain=None):
    t = x.shape[0]
    nj = D_FF // FFN_TF
    in_specs = [
        pl.BlockSpec((FFN_TM, D_MODEL), lambda i, j: (i, 0)),
        pl.BlockSpec((1, D_MODEL), lambda i, j: (0, 0)),
        pl.BlockSpec((None, D_MODEL, FFN_TF), lambda i, j: (layer_in, 0, j)),
        pl.BlockSpec((None, D_MODEL, FFN_TF), lambda i, j: (layer_in, 0, nj + j)),
        pl.BlockSpec((None, FFN_TF, D_MODEL), lambda i, j: (layer_out, j, 0)),
    ]
    args = [x, gain.reshape(1, D_MODEL), w_in, w_in, w_out]
    if final_gain is not None:
        in_specs.append(pl.BlockSpec((1, D_MODEL), lambda i, j: (0, 0)))
        args.append(final_gain.reshape(1, D_MODEL))
    return pl.pallas_call(
        functools.partial(_ffn_kernel, final_norm=final_gain is not None),
        grid=(t // FFN_TM, nj),
        in_specs=in_specs,
        out_specs=pl.BlockSpec((FFN_TM, D_MODEL), lambda i, j: (i, 0)),
        out_shape=jax.ShapeDtypeStruct((t, D_MODEL), F32),
        scratch_shapes=[pltpu.VMEM((FFN_TM, D_MODEL), BF16)],
        compiler_params=_params(("parallel", "arbitrary")),
        name="swiglu_ffn",
    )(*args)


SIDE_CHUNKS = 32


def _side_cast_specs(jobs, step_of):
    in_specs, out_specs, out_shapes, args = [], [], [], []
    for w, layer in jobs:
        _, r, c = w.shape
        rows = r // SIDE_CHUNKS
        assert rows * SIDE_CHUNKS == r and rows % 16 == 0

        def chunk(*g):
            return jnp.minimum(step_of(*g), SIDE_CHUNKS - 1)

        in_specs.append(pl.BlockSpec((None, rows, c), lambda *g, layer=layer: (layer, chunk(*g), 0)))
        out_specs.append(pl.BlockSpec((None, rows, c), lambda *g: (0, chunk(*g), 0)))
        out_shapes.append(jax.ShapeDtypeStruct((1, r, c), BF16))
        args.append(w)
    return in_specs, out_specs, out_shapes, args


def _side_cast(srcs, dsts):
    for src, dst in zip(srcs, dsts):
        dst[...] = src[...].astype(BF16)


def _proj_kernel(*refs, epilogue, h_mode, n_side):
    refs = list(refs)
    if h_mode == "input":
        h_ref = refs.pop(0)
    else:
        x_ref, g_ref = refs.pop(0), refs.pop(0)
    w_ref = refs.pop(0)
    if epilogue == "rotary":
        cos_ref, sin_ref = refs.pop(0), refs.pop(0)
    elif epilogue == "gate":
        gn_ref = refs.pop(0)
    side_srcs = [refs.pop(0) for _ in range(n_side)]
    o_ref = refs.pop(0)
    if h_mode == "emit":
        h_ref = refs.pop(0)
    side_dsts = [refs.pop(0) for _ in range(n_side)]
    if h_mode == "scratch":
        h_ref = refs.pop(0)
    n = pl.program_id(1)
    _side_cast(side_srcs, side_dsts)

    if h_mode != "input":
        @pl.when(n == 0)
        def _():
            h_ref[...] = _rmsnorm(x_ref[...], g_ref[...]).astype(BF16)

    y = jnp.dot(h_ref[...], w_ref[...], preferred_element_type=F32)
    if epilogue == "plain":
        o_ref[...] = y.astype(BF16)
        return
    if epilogue == "qscale":
        o_ref[...] = (y * jnp.where(n == 0, NA_Q_SCALE, 1.0).astype(F32)).astype(BF16)
        return
    if epilogue == "gate":
        o_ref[...] = (_silu(y) * gn_ref[...]).astype(BF16)
        return
    cos = cos_ref[...]
    sin = sin_ref[...]
    scale = jnp.where(n < pl.num_programs(1) // 2, 1.0, RET_DK ** -0.5).astype(F32)
    half = RET_DK // 2
    for hh in range(PROJ_TN // RET_DK):
        x1 = y[:, hh * RET_DK: hh * RET_DK + half]
        x2 = y[:, hh * RET_DK + half: (hh + 1) * RET_DK]
        o_ref[:, hh * RET_DK: hh * RET_DK + half] = ((x1 * cos - x2 * sin) * scale).astype(BF16)
        o_ref[:, hh * RET_DK + half: (hh + 1) * RET_DK] = ((x1 * sin + x2 * cos) * scale).astype(BF16)


def _proj(x, gain, w, layer, cols, rotary=None, gate_gain=None, emit_h=False, side=(), q_scale=False):
    t = x.shape[0]
    n_out = cols[1] - cols[0]
    col0 = cols[0] // PROJ_TN
    grid = (t // PROJ_TM, n_out // PROJ_TN)
    row_spec = pl.BlockSpec((PROJ_TM, D_MODEL), lambda i, n: (i, 0))
    h_mode = "input" if gain is None else ("emit" if emit_h else "scratch")
    in_specs = [row_spec]
    args = [x]
    if gain is not None:
        in_specs.append(pl.BlockSpec((1, D_MODEL), lambda i, n: (0, 0)))
        args.append(gain.reshape(1, D_MODEL))
    in_specs.append(pl.BlockSpec((None, D_MODEL, PROJ_TN), lambda i, n: (layer, 0, col0 + n)))
    args.append(w)
    epilogue = "qscale" if q_scale else "plain"
    assert not q_scale or PROJ_TN == D_MODEL
    if rotary is not None:
        epilogue = "rotary"
        seq_tiles = SEQ // PROJ_TM
        tab_spec = pl.BlockSpec((PROJ_TM, RET_DK // 2), lambda i, n: (i % seq_tiles, 0))
        in_specs += [tab_spec, tab_spec]
        args += list(rotary)
    elif gate_gain is not None:
        epilogue = "gate"
        in_specs.append(pl.BlockSpec((1, PROJ_TN), lambda i, n: (0, n)))
        args.append(gate_gain.reshape(1, n_out).astype(F32))
    assert not side or grid[0] * grid[1] >= SIDE_CHUNKS
    s_in, s_out, s_shapes, s_args = _side_cast_specs(side, lambda i, n: i * grid[1] + n)
    out_specs = [pl.BlockSpec((PROJ_TM, PROJ_TN), lambda i, n: (i, n))]
    out_shape = [jax.ShapeDtypeStruct((t, n_out), BF16)]
    if h_mode == "emit":
        out_specs.append(row_spec)
        out_shape.append(jax.ShapeDtypeStruct((t, D_MODEL), BF16))
    outs = pl.pallas_call(
        functools.partial(_proj_kernel, epilogue=epilogue, h_mode=h_mode, n_side=len(side)),
        grid=grid,
        in_specs=in_specs + s_in,
        out_specs=out_specs + s_out,
        out_shape=out_shape + s_shapes,
        scratch_shapes=[pltpu.VMEM((PROJ_TM, D_MODEL), BF16)] if h_mode == "scratch" else [],
        compiler_params=_params(("parallel", "arbitrary")),
        name="proj_" + epilogue + "_h" + h_mode,
    )(*args, *s_args)
    n_main = len(out_specs)
    return outs[0], (outs[1] if h_mode == "emit" else None), list(outs[n_main:])


def _out_kernel(a_ref, w_ref, x_ref, *rest, n_side):
    side_srcs, o_ref, side_dsts = rest[:n_side], rest[n_side], rest[n_side + 1:]
    _side_cast(side_srcs, side_dsts)
    o_ref[...] = x_ref[...] + jnp.dot(a_ref[...], w_ref[...], preferred_element_type=F32)


def _out_proj(a, w, layer, x, side=()):
    t, k = a.shape
    steps = t // OUT_TM
    assert not side or steps >= SIDE_CHUNKS
    s_in, s_out, s_shapes, s_args = _side_cast_specs(side, lambda i: i)
    outs = pl.pallas_call(
        functools.partial(_out_kernel, n_side=len(side)),
        grid=(steps,),
        in_specs=[
            pl.BlockSpec((OUT_TM, k), lambda i: (i, 0)),
            pl.BlockSpec((None, k, D_MODEL), lambda i: (layer, 0, 0), pipeline_mode=pl.Buffered(1)),
            pl.BlockSpec((OUT_TM, D_MODEL), lambda i: (i, 0)),
        ] + s_in,
        out_specs=[pl.BlockSpec((OUT_TM, D_MODEL), lambda i: (i, 0))] + s_out,
        out_shape=[jax.ShapeDtypeStruct((t, D_MODEL), F32)] + s_shapes,
        compiler_params=_params(("parallel",)),
        name="out_proj_residual",
    )(a, w, x, *s_args)
    return outs[0], list(outs[1:])


SCAN_NBLK = SEQ // SCAN_TC
SCAN_CHUNKS = SCAN_TC // SCAN_CHUNK


def _scan_kernel(dec_ref, q_ref, k_ref, v_ref, gf_ref, gb_ref, tab_ref, o_ref, r_ref, ob_ref):
    h = pl.program_id(1)
    t = pl.program_id(2)
    c_len = SCAN_CHUNK

    @pl.when((t == 0) | (t == SCAN_NBLK))
    def _():
        r_ref[...] = jnp.zeros_like(r_ref)

    def run(backward):
        inner = tab_ref[0]
        cross = jnp.concatenate([tab_ref[1]] * (RET_DV // c_len), axis=1)
        into = jnp.concatenate([tab_ref[2]] * (RET_DK // c_len), axis=1)
        dec = dec_ref[1 if backward else 0, h]
        gate_ref = gb_ref if backward else gf_ref
        blk = (SCAN_NBLK - 1 - t) if backward else (t - SCAN_NBLK)
        order = range(SCAN_CHUNKS - 1, -1, -1) if backward else range(SCAN_CHUNKS)
        for c in order:
            rows = pl.ds(c * c_len, c_len)
            seq_rows = pl.ds(pl.multiple_of(blk * SCAN_TC + c * c_len, c_len), c_len)
            q = q_ref[rows, :]
            k = k_ref[rows, :]
            v = v_ref[rows, :]
            s = lax.dot_general(q, k, (((1,), (1,)), ((), ())), preferred_element_type=F32) * inner
            kd = (k.astype(F32) * into).astype(BF16)
            r = r_ref[...]
            sv = jnp.dot(jnp.concatenate([s.astype(BF16), kd.T], axis=0), v, preferred_element_type=F32)
            o = sv[:c_len] + jnp.dot(q, r.astype(BF16), preferred_element_type=F32) * cross
            r_ref[...] = r * dec + sv[c_len:]
            on = o * lax.rsqrt(jnp.mean(o * o, axis=-1, keepdims=True) + RMS_EPS)
            m = gate_ref[rows, :].astype(F32) * on
            if backward:
                ob_ref[seq_rows, :] = m
            else:
                o_ref[rows, :] = (m + ob_ref[seq_rows, :]).astype(BF16)

    @pl.when(t < SCAN_NBLK)
    def _():
        o_ref[...] = jnp.zeros_like(o_ref)
        run(True)

    @pl.when(t >= SCAN_NBLK)
    def _():
        run(False)


def _retention_scan(qk, v, gates, tabs, dec):
    t = qk.shape[0]
    batch = t // SEQ
    nb = SCAN_NBLK

    def bwd_blk(s):
        return jnp.where(s < nb, nb - 1 - s, 0)

    def fwd_blk(s):
        return jnp.where(s < nb, 0, s - nb)

    def cur_blk(s):
        return jnp.where(s < nb, nb - 1 - s, s - nb)

    nh = RET_HEADS
    in_specs = [
        pl.BlockSpec(memory_space=pltpu.SMEM),
        pl.BlockSpec((SCAN_TC, RET_DK), lambda b, h, s: (b * nb + cur_blk(s), h)),
        pl.BlockSpec((SCAN_TC, RET_DK), lambda b, h, s: (b * nb + cur_blk(s), nh + h)),
        pl.BlockSpec((SCAN_TC, RET_DV), lambda b, h, s: (b * nb + cur_blk(s), h)),
        pl.BlockSpec((SCAN_TC, RET_DV), lambda b, h, s: (b * nb + fwd_blk(s), h)),
        pl.BlockSpec((SCAN_TC, RET_DV), lambda b, h, s: (b * nb + bwd_blk(s), nh + h)),
        pl.BlockSpec((None, None, 3, SCAN_CHUNK, SCAN_CHUNK),
                     lambda b, h, s: (jnp.where(s < nb, 1, 0), h, 0, 0, 0)),
    ]
    return pl.pallas_call(
        _scan_kernel,
        grid=(batch, RET_HEADS, 2 * nb),
        in_specs=in_specs,
        out_specs=pl.BlockSpec((SCAN_TC, RET_DV), lambda b, h, s: (b * nb + fwd_blk(s), h)),
        out_shape=jax.ShapeDtypeStruct((t, RET_VDIM), BF16),
        scratch_shapes=[pltpu.VMEM((RET_DK, RET_DV), F32), pltpu.VMEM((SEQ, RET_DV), F32)],
        compiler_params=_params(("arbitrary", "arbitrary", "arbitrary")),
        name="retention_scan",
    )(dec, qk, qk, v, gates, gates, tabs)


def _retention_tables(decay_f, decay_b):
    c = SCAN_CHUNK
    pos = jnp.arange(c, dtype=F32)
    ones = jnp.ones((1, 1, c), F32)

    def tables(log_gamma, backward):
        lg = log_gamma[:, None, None]
        diff = pos[:, None] - pos[None, :]
        if backward:
            diff = -diff
        inner = jnp.where(diff[None] >= 0, jnp.exp(lg * jnp.maximum(diff, 0.0)[None]), 0.0)
        cross_pos = (c - pos) if backward else (pos + 1.0)
        into_pos = pos if backward else (c - 1.0 - pos)
        cross = jnp.exp(log_gamma[:, None] * cross_pos[None])[:, :, None] * ones
        into = jnp.exp(log_gamma[:, None] * into_pos[None])[:, :, None] * ones
        return jnp.stack([inner, cross, into], axis=1), jnp.exp(log_gamma * c)

    lg_f = jnp.log1p(-jnp.exp(decay_f.astype(F32)))
    lg_b = jnp.log1p(-jnp.exp(decay_b.astype(F32)))
    tab_f, dec_f = tables(lg_f, False)
    tab_b, dec_b = tables(lg_b, True)
    return jnp.stack([tab_f, tab_b]), jnp.stack([dec_f, dec_b])


def _rotary_tables():
    d = RET_DK
    inv = ROPE_BASE ** (-jnp.arange(0, d, 2, dtype=F32) / d)
    ang = jnp.arange(SEQ, dtype=F32)[:, None] * inv[None, :]
    return jnp.cos(ang), jnp.sin(ang)


NA_QROWS = 4
NA_GROUPS = GRID_ROWS // NA_QROWS
NA_SLAB_ROWS = NA_QROWS + NA_KH
NA_QTOK = NA_QROWS * GRID_W
NA_SLAB = NA_SLAB_ROWS * GRID_W


NA_GEOMETRIES = (
    lambda rq, rk: (rk < NA_KH, rk - rq + NA_KH - 1),
    lambda rq, rk: (rq <= rk < rq + NA_KH, rk - rq + NA_KH // 2 - 1),
    lambda rq, rk: (rk >= NA_QROWS, rk - rq - 1),
)


def _na_kernel(q_ref, k_ref, v_ref, toe_ref, o_ref, s_ref, bias_ref):

    @pl.when(pl.program_id(1) == 0)
    def _():
        first_row = lax.broadcasted_iota(jnp.int32, (GRID_W, 2 * GRID_W), 1) < GRID_W
        masked = jnp.full((GRID_W, 2 * GRID_W), MASK_VALUE, F32)
        for kind, geometry in enumerate(NA_GEOMETRIES):
            for rq in range(NA_QROWS):
                for rk in range(0, NA_SLAB_ROWS, 2):
                    (ok0, dr0), (ok1, dr1) = geometry(rq, rk), geometry(rq, rk + 1)
                    tile = jnp.where(first_row, toe_ref[dr0] if ok0 else masked,
                                     toe_ref[dr1] if ok1 else masked) if (ok0 or ok1) else masked
                    bias_ref[kind, rq * GRID_W:(rq + 1) * GRID_W, rk * GRID_W:(rk + 2) * GRID_W] = tile

    def windows(g):
        s0 = jnp.clip(NA_QROWS * g - NA_KH // 2, 0, GRID_ROWS - NA_SLAB_ROWS)
        rows = pl.ds(pl.multiple_of(g * NA_QTOK, NA_QTOK), NA_QTOK)
        slab = pl.ds(pl.multiple_of(s0 * GRID_W, NA_QTOK), NA_SLAB)
        return rows, slab

    def scores(g, slot):
        rows, slab = windows(g)
        kind = jnp.where(g == 0, 0, jnp.where(g == NA_GROUPS - 1, 2, 1))
        s = lax.dot_general(q_ref[rows, :], k_ref[slab, :], (((1,), (1,)), ((), ())),
                            preferred_element_type=F32)
        s_ref[slot] = s + bias_ref[kind]

    def attend(g, slot):
        rows, slab = windows(g)
        s = s_ref[slot]
        e = jnp.exp2(s - jnp.max(s, axis=-1, keepdims=True))
        o = jnp.dot(e.astype(BF16), v_ref[slab, :], preferred_element_type=F32)
        o_ref[rows, :] = (o * (1.0 / jnp.sum(e, axis=-1, keepdims=True))).astype(BF16)

    scores(0, 0)

    def pair(i, carry):
        g = 2 * i
        scores(g + 1, 1)
        attend(g, 0)
        scores(g + 2, 0)
        attend(g + 1, 1)
        return carry

    lax.fori_loop(0, NA_GROUPS // 2 - 1, pair, 0, unroll=True)
    scores(NA_GROUPS - 1, 1)
    attend(NA_GROUPS - 2, 0)
    attend(NA_GROUPS - 1, 1)


def _na_bias_tables(rpb):
    pad = GRID_W - NA_KW
    padded = jnp.pad(rpb.astype(F32), ((0, 0), (0, 0), (pad, pad)))
    toe = jnp.stack([padded[:, :, GRID_W - 1 - c: 2 * GRID_W - 1 - c] for c in range(GRID_W)], axis=2)
    cq = np.arange(GRID_W)
    ck = np.arange(GRID_W)
    qwin = np.clip(cq - NA_KW // 2, 0, GRID_W - NA_KW)
    col_ok = (ck[None, :] >= qwin[:, None]) & (ck[None, :] < qwin[:, None] + NA_KW)
    toe = jnp.where(col_ok[None, None], toe * LOG2_E, MASK_VALUE)
    return jnp.concatenate([toe, toe], axis=-1)


def _neighbourhood_attention(qkv, bias):
    t = qkv.shape[0]
    batch = t // SEQ
    hd = NA_HEAD_DIM
    return pl.pallas_call(
        _na_kernel,
        grid=(NA_HEADS, batch),
        in_specs=[
            pl.BlockSpec((SEQ, hd), lambda h, b: (b, h)),
            pl.BlockSpec((SEQ, hd), lambda h, b: (b, NA_HEADS + h)),
            pl.BlockSpec((SEQ, hd), lambda h, b: (b, 2 * NA_HEADS + h)),
            pl.BlockSpec((None, 2 * NA_KH - 1, GRID_W, 2 * GRID_W), lambda h, b: (h, 0, 0, 0)),
        ],
        out_specs=pl.BlockSpec((SEQ, hd), lambda h, b: (b, h)),
        out_shape=jax.ShapeDtypeStruct((t, D_MODEL), BF16),
        scratch_shapes=[pltpu.VMEM((2, NA_QTOK, NA_SLAB), F32), pltpu.VMEM((3, NA_QTOK, NA_SLAB), F32)],
        compiler_params=_params(("arbitrary", "arbitrary")),
        name="neighbourhood_attention",
    )(qkv, qkv, qkv, bias)


def _trunk(x, weights, tables, raw):
    assert DEPTH == 2

    def side(names):
        return [(raw[name], layer) for name, layer in names] if raw else []

    def publish(names, casted):
        for (name, layer), w in zip(names, casted):
            weights[name, layer] = (w, 0)

    for i in range(DEPTH):
        x = _ffn(x, tables["norm_ffn1"][i], *weights["ffn1_w_in", i], *weights["ffn1_w_out", i])
        j = i // 2
        if i % 2 == 0:
            w_in, lj = weights["ret_w_in", j]
            qk, h, _ = _proj(x, tables["norm_mix"][i], w_in, lj, (0, 2 * D_MODEL), rotary=tables["rotary"],
                             emit_h=True)
            names = [("ffn2_w_in", i), ("ffn2_w_out", i), ("na_w_in", j)]
            v, _, casted = _proj(h, None, w_in, lj, (2 * D_MODEL, 2 * D_MODEL + RET_VDIM), side=side(names))
            publish(names, casted)
            names = [("ffn1_w_in", i + 1), ("ffn1_w_out", i + 1), ("na_w_out", j), ("ret_w_out", j)]
            gates, _, casted = _proj(h, None, w_in, lj, (2 * D_MODEL + RET_VDIM, RET_IN),
                                     gate_gain=jnp.concatenate([tables["ret_gn_f"][j], tables["ret_gn_b"][j]]),
                                     side=side(names))
            publish(names, casted)
            tabs, dec = tables["ret_tables"][j]
            mixed = _retention_scan(qk, v, gates, tabs, dec)
            names = [("ffn2_w_in", i + 1)]
            x, casted = _out_proj(mixed, *weights["ret_w_out", j], x, side=side(names))
            publish(names, casted)
        else:
            names = [("ffn2_w_out", i)]
            qkv, _, casted = _proj(x, tables["norm_mix"][i], *weights["na_w_in", j], (0, 3 * D_MODEL),
                                   side=side(names), q_scale=True)
            publish(names, casted)
            att = _neighbourhood_attention(qkv, tables["na_bias"][j])
            x, _ = _out_proj(att, *weights["na_w_out", j], x)
        x = _ffn(x, tables["norm_ffn2"][i], *weights["ffn2_w_in", i], *weights["ffn2_w_out", i],
                 final_gain=tables["norm_final"] if i == DEPTH - 1 else None)
    return x


def kernel(x_prompt, x_sample, norm_ffn1, ffn1_w_in, ffn1_w_out, norm_mix, norm_ffn2, ffn2_w_in, ffn2_w_out,
           ret_w_in, ret_w_out, ret_decay_f, ret_decay_b, ret_gn_f, ret_gn_b, na_w_in, na_w_out, na_rpb,
           norm_final):
    n_ret = ret_w_in.shape[0]
    n_na = na_w_in.shape[0]
    tables = dict(
        norm_ffn1=norm_ffn1, norm_mix=norm_mix, norm_ffn2=norm_ffn2, norm_final=norm_final,
        ret_gn_f=ret_gn_f, ret_gn_b=ret_gn_b,
        ret_tables=[_retention_tables(ret_decay_f[j], ret_decay_b[j]) for j in range(n_ret)],
        rotary=_rotary_tables(),
        na_bias=[_na_bias_tables(na_rpb[j]) for j in range(n_na)],
    )
    weights = {}
    up_front = dict(ffn1_w_in=ffn1_w_in[:1], ffn1_w_out=ffn1_w_out[:1], ret_w_in=ret_w_in)
    for name, w in up_front.items():
        stack = w.astype(BF16)
        for layer in range(w.shape[0]):
            weights[name, layer] = (stack, layer)
    raw = dict(ffn1_w_in=ffn1_w_in, ffn1_w_out=ffn1_w_out, ffn2_w_in=ffn2_w_in, ffn2_w_out=ffn2_w_out,
               ret_w_out=ret_w_out, na_w_in=na_w_in, na_w_out=na_w_out)
    outs = []
    for x in (x_prompt, x_sample):
        y = _trunk(x.reshape(-1, D_MODEL), weights, tables, raw)
        raw = None
        outs.append(y.reshape(x.shape))
    return tuple(outs)
```
